```python
import math
import jax
import jax.numpy as jnp
from jax import lax
import numpy as np

D_MODEL = 4096
BATCH = 1
SEQ = 8192
DEPTH = 2

MIX_WIDTH = D_MODEL
ATTN_WIDTH = MIX_WIDTH // 2
ATTN_HEAD_DIM = 128
ATTN_HEADS = ATTN_WIDTH // ATTN_HEAD_DIM
DILATION_PATTERNS = ((128, 1), (512, 4), (2048, 16))
ATTN_Q_BLOCK = 64
REL_BUCKETS = 32
REL_MAX_EXACT = REL_BUCKETS // 2
REL_MAX_DISTANCE = 2048
MLSTM_WIDTH = MIX_WIDTH - ATTN_WIDTH
MLSTM_HEADS = 4
MLSTM_V_DIM = MLSTM_WIDTH // MLSTM_HEADS
MLSTM_QK_DIM = MLSTM_V_DIM // 2
MLSTM_QK_WIDTH = MLSTM_HEADS * MLSTM_QK_DIM
MLSTM_CHUNK = 64
CONV_WIDTH = 4
NORM_EPS = 1e-6
SPLIT_SIZES = (ATTN_WIDTH, ATTN_WIDTH, ATTN_WIDTH, ATTN_WIDTH,
               MLSTM_QK_WIDTH, MLSTM_QK_WIDTH, MLSTM_WIDTH, MLSTM_WIDTH, MLSTM_WIDTH,
               MLSTM_HEADS, MLSTM_HEADS)
IN_WIDTH = sum(SPLIT_SIZES)

kernel_name = 'hymba_dilated_attn_mlstm_hybrid'


def rmsnorm(x, g):
    xf = x.astype(jnp.float32)
    y = xf * lax.rsqrt(jnp.mean(xf * xf, axis=-1, keepdims=True) + NORM_EPS)
    return (y * g.astype(jnp.float32)).astype(x.dtype)


def rel_buckets(dist):
    n = dist.astype(np.float32)
    large = REL_MAX_EXACT + np.floor(
        np.log(np.maximum(n, 1.0) / REL_MAX_EXACT)
        / np.log(REL_MAX_DISTANCE / REL_MAX_EXACT) * (REL_BUCKETS - REL_MAX_EXACT))
    large = np.minimum(large, REL_BUCKETS - 1)
    return np.where(n < REL_MAX_EXACT, n, large).astype(np.int32)


def dilated_attention(q, k, v, rel_bias):
    B, S, H, Dh = q.shape
    q = q * (Dh ** -0.5)
    offs = [np.arange(w // d + 1, dtype=np.int32) * d for w, d in DILATION_PATTERNS]
    biases = [rel_bias[rel_buckets(o)].T.astype(jnp.float32) for o in offs]

    def block(bi):
        start = bi * ATTN_Q_BLOCK
        t = start + jnp.arange(ATTN_Q_BLOCK, dtype=jnp.int32)
        qb = lax.dynamic_slice_in_dim(q, start, ATTN_Q_BLOCK, axis=1)
        outs, lses = [], []
        for off, bias in zip(offs, biases):
            idx = t[:, None] - jnp.asarray(off)[None, :]
            valid = idx >= 0
            idx = jnp.maximum(idx, 0)
            kg = k[:, idx]
            vg = v[:, idx]
            s = jnp.einsum('bqhd,bqjhd->bhqj', qb, kg).astype(jnp.float32) + bias[None, :, None, :]
            s = jnp.where(valid[None, None], s, -jnp.inf)
            m = jnp.max(s, axis=-1, keepdims=True)
            p = jnp.exp(s - m)
            l = jnp.sum(p, axis=-1)
            o = jnp.einsum('bhqj,bqjhd->bhqd', p, vg.astype(jnp.float32)) / l[..., None]
            outs.append(o)
            lses.append(m[..., 0] + jnp.log(l))
        wts = jax.nn.softmax(jnp.stack(lses), axis=0)
        o = jnp.einsum('pbhq,pbhqd->bqhd', wts, jnp.stack(outs))
        return o.astype(q.dtype)

    ob = lax.map(block, jnp.arange(S // ATTN_Q_BLOCK, dtype=jnp.int32))
    return ob.transpose(1, 0, 2, 3, 4).reshape(B, S, H, Dh)


def causal_conv(x, w):
    S = x.shape[1]
    xp = jnp.pad(x, ((0, 0), (CONV_WIDTH - 1, 0), (0, 0)))
    return sum(xp[:, j:j + S] * w[j] for j in range(CONV_WIDTH))


def mlstm_chunkwise(q, k, v, ig, fg):
    B, S, H, Dk = q.shape
    Dv = v.shape[-1]
    L = MLSTM_CHUNK
    NC = S // L
    f32 = jnp.float32
    q = q.astype(f32)
    k = k.astype(f32) * (Dk ** -0.5)
    v = v.astype(f32)
    log_f = jax.nn.log_sigmoid(fg.astype(f32))
    log_i = ig.astype(f32)

    def chunks(a):
        return jnp.moveaxis(a.reshape((B, NC, L) + a.shape[2:]), 1, 0).swapaxes(2, 3)

    qc, kc, vc = chunks(q), chunks(k), chunks(v)
    lfc, lic = chunks(log_f), chunks(log_i)
    bc = jnp.cumsum(lfc, axis=-1)
    tri = jnp.tril(jnp.ones((L, L), dtype=bool))

    def step(carry, inp):
        C, n, m = carry
        qx, kx, vx, b, li = inp
        D = b[..., :, None] - b[..., None, :] + li[..., None, :]
        D = jnp.where(tri, D, -jnp.inf)
        m_inter = b + m[..., None]
        m_t = jnp.maximum(m_inter, jnp.max(D, axis=-1))
        w_inter = jnp.exp(m_inter - m_t)
        Sqk = jnp.einsum('bhld,bhsd->bhls', qx, kx) * jnp.exp(D - m_t[..., None])
        num = w_inter[..., None] * jnp.einsum('bhld,bhdv->bhlv', qx, C) + jnp.einsum('bhls,bhsv->bhlv', Sqk, vx)
        nq = w_inter * jnp.einsum('bhld,bhd->bhl', qx, n) + jnp.sum(Sqk, axis=-1)
        h = num / jnp.maximum(jnp.abs(nq), jnp.exp(-m_t))[..., None]
        bL = b[..., -1]
        g = bL[..., None] - b + li
        m_new = jnp.maximum(bL + m, jnp.max(g, axis=-1))
        a = jnp.exp(bL + m - m_new)
        ws = jnp.exp(g - m_new[..., None])
        C_new = a[..., None, None] * C + jnp.einsum('bhs,bhsd,bhsv->bhdv', ws, kx, vx)
        n_new = a[..., None] * n + jnp.einsum('bhs,bhsd->bhd', ws, kx)
        return (C_new, n_new, m_new), h

    init = (jnp.zeros((B, H, Dk, Dv), f32), jnp.zeros((B, H, Dk), f32), jnp.zeros((B, H), f32))
    _, hs = lax.scan(step, init, (qc, kc, vc, bc, lic))
    return hs.transpose(1, 0, 3, 2, 4).reshape(B, S, H, Dv)


def setup_inputs(seed: int = 0) -> dict:
    key = jax.random.key(seed)
    ks = jax.random.split(key, 10)
    f32 = jnp.float32
    x = jax.random.normal(ks[0], (BATCH, SEQ, D_MODEL), f32)
    norm_g = 1.0 + 0.05 * jax.random.normal(ks[1], (DEPTH, D_MODEL), f32)
    w_in = jax.random.normal(ks[2], (DEPTH, D_MODEL, IN_WIDTH), f32) * D_MODEL ** -0.5
    b_i = 0.1 * jax.random.normal(ks[3], (DEPTH, MLSTM_HEADS), f32)
    b_f = jnp.linspace(3.0, 6.0, MLSTM_HEADS, dtype=f32)[None, :] + 0.1 * jax.random.normal(ks[4], (DEPTH, MLSTM_HEADS), f32)
    b_gate = jnp.concatenate([b_i, b_f], axis=-1)
    conv_w = jax.random.normal(ks[5], (DEPTH, CONV_WIDTH, 2 * MLSTM_QK_WIDTH), f32) * CONV_WIDTH ** -0.5
    mlstm_norm_g = 1.0 + 0.05 * jax.random.normal(ks[6], (DEPTH, MLSTM_WIDTH), f32)
    w_out = jax.random.normal(ks[7], (DEPTH, MIX_WIDTH, D_MODEL), f32) * (0.5 * MIX_WIDTH ** -0.5)
    rel_bias = 0.5 * jax.random.normal(ks[8], (REL_BUCKETS, ATTN_HEADS), f32)
    final_g = 1.0 + 0.05 * jax.random.normal(ks[9], (D_MODEL,), f32)
    return {'x': x, 'norm_g': norm_g, 'w_in': w_in, 'b_gate': b_gate, 'conv_w': conv_w,
            'mlstm_norm_g': mlstm_norm_g, 'w_out': w_out, 'rel_bias': rel_bias, 'final_g': final_g}


def reference(x, norm_g, w_in, b_gate, conv_w, mlstm_norm_g, w_out, rel_bias, final_g):
    B, S, _ = x.shape
    split_points = tuple(int(p) for p in np.cumsum(SPLIT_SIZES)[:-1])
    for l in range(DEPTH):
        h = rmsnorm(x, norm_g[l])
        u = jnp.einsum('bsd,de->bse', h, w_in[l])
        qa, ka, va, za, qm, km, vm, om, zm, ig, fg = jnp.split(u, split_points, axis=-1)
        hd = (B, S, ATTN_HEADS, ATTN_HEAD_DIM)
        ya = dilated_attention(qa.reshape(hd), ka.reshape(hd), va.reshape(hd), rel_bias)
        ya = ya.reshape(B, S, ATTN_WIDTH) * jax.nn.silu(za)
        qk = jax.nn.silu(causal_conv(jnp.concatenate([qm, km], axis=-1), conv_w[l]))
        qm, km = jnp.split(qk, 2, axis=-1)
        ig = ig.astype(jnp.float32) + b_gate[l, :MLSTM_HEADS].astype(jnp.float32)
        fg = fg.astype(jnp.float32) + b_gate[l, MLSTM_HEADS:].astype(jnp.float32)
        hm = mlstm_chunkwise(qm.reshape(B, S, MLSTM_HEADS, MLSTM_QK_DIM),
                             km.reshape(B, S, MLSTM_HEADS, MLSTM_QK_DIM),
                             vm.reshape(B, S, MLSTM_HEADS, MLSTM_V_DIM), ig, fg)
        hm = hm * lax.rsqrt(jnp.mean(hm * hm, axis=-1, keepdims=True) + NORM_EPS)
        hm = hm * mlstm_norm_g[l].astype(jnp.float32).reshape(MLSTM_HEADS, MLSTM_V_DIM)
        ym = hm.reshape(B, S, MLSTM_WIDTH).astype(x.dtype) * jax.nn.sigmoid(om) * jax.nn.silu(zm)
        y = jnp.concatenate([ya, ym], axis=-1)
        x = x + jnp.einsum('bse,ed->bsd', y, w_out[l]).astype(x.dtype)
    return rmsnorm(x, final_g)
```

```python
import functools
import math

import jax
import jax.numpy as jnp
import numpy as np
from jax import lax
from jax.experimental import pallas as pl
from jax.experimental.pallas import tpu as pltpu

D_MODEL = 4096
DEPTH = 2
ATTN_WIDTH = 2048
ATTN_HEAD_DIM = 128
ATTN_HEADS = ATTN_WIDTH // ATTN_HEAD_DIM
DILATION_PATTERNS = ((128, 1), (512, 4), (2048, 16))
REL_BUCKETS = 32
REL_MAX_EXACT = REL_BUCKETS // 2
REL_MAX_DISTANCE = 2048
MLSTM_WIDTH = 2048
MLSTM_HEADS = 4
MLSTM_V_DIM = MLSTM_WIDTH // MLSTM_HEADS
MLSTM_QK_DIM = MLSTM_V_DIM // 2
MLSTM_QK_WIDTH = MLSTM_HEADS * MLSTM_QK_DIM
CONV_WIDTH = 4
NORM_EPS = 1e-6
ATTN_COLS = 4 * ATTN_WIDTH
MLSTM_COLS = 2 * MLSTM_QK_WIDTH + 3 * MLSTM_WIDTH
GATE_COL0 = ATTN_COLS + MLSTM_COLS

LANES = 128
SUBLANES = 8
VMEM_LIMIT_BYTES = 56 * 1024 * 1024

NORM_ROWS = 256
MM_TM = 1024
MM_TN = 1024
OUT_TM = 512
OUT_TN = 1024
MLSTM_CHUNK = 256
ATTN_TILE = 256
ATTN_MAX_DIST = max(w for w, _ in DILATION_PATTERNS)
ATTN_KEY_TILES = ATTN_MAX_DIST // ATTN_TILE + 1
CONV_HALO = SUBLANES
MASKED = -1e30

F32 = jnp.float32
BF16 = jnp.bfloat16


def _params(*semantics):
    return pltpu.CompilerParams(dimension_semantics=semantics,
                                vmem_limit_bytes=VMEM_LIMIT_BYTES)


def _sigmoid(x):
    return 1.0 / (1.0 + jnp.exp(-x))


def _silu(x):
    return x * _sigmoid(x)


def _log_sigmoid(x):
    return jnp.minimum(x, 0.0) - jnp.log1p(jnp.exp(-jnp.abs(x)))


def _rmsnorm_kernel(x_ref, g_ref, o_ref):
    x = x_ref[...]
    ms = jnp.mean(x * x, axis=-1, keepdims=True)
    o_ref[...] = (x * lax.rsqrt(ms + NORM_EPS) * g_ref[...]).astype(o_ref.dtype)


def _rmsnorm(x, g, out_dtype):
    s, d = x.shape
    return pl.pallas_call(
        _rmsnorm_kernel,
        out_shape=jax.ShapeDtypeStruct((s, d), out_dtype),
        grid=(s // NORM_ROWS,),
        in_specs=[pl.BlockSpec((NORM_ROWS, d), lambda i: (i, 0)),
                  pl.BlockSpec((1, d), lambda i: (0, 0))],
        out_specs=pl.BlockSpec((NORM_ROWS, d), lambda i: (i, 0)),
        compiler_params=_params("parallel"),
        name="rmsnorm",
    )(x, g.reshape(1, d))


def _matmul_kernel(a_ref, w_ref, o_ref):
    o_ref[...] = jnp.dot(a_ref[...], w_ref[...],
                         preferred_element_type=F32).astype(o_ref.dtype)


def _in_proj(h, w, col0, n_cols, out_dtype, name):
    s, k = h.shape
    j0 = col0 // MM_TN
    return pl.pallas_call(
        _matmul_kernel,
        out_shape=jax.ShapeDtypeStruct((s, n_cols), out_dtype),
        grid=(s // MM_TM, n_cols // MM_TN),
        in_specs=[pl.BlockSpec((MM_TM, k), lambda i, j: (i, 0)),
                  pl.BlockSpec((k, MM_TN), lambda i, j: (0, j + j0))],
        out_specs=pl.BlockSpec((MM_TM, MM_TN), lambda i, j: (i, j)),
        compiler_params=_params("parallel", "parallel"),
        name=name,
    )(h, w)


def _gates_kernel(h_ref, wc_ref, wr_ref, bc_ref, br_ref, gc_ref, gr_ref):
    L = h_ref.shape[0]
    h = h_ref[...]
    gc = jnp.dot(h, wc_ref[...], preferred_element_type=F32) + bc_ref[...]
    gr = lax.dot_general(wr_ref[...], h, (((1,), (1,)), ((), ())),
                         preferred_element_type=F32)[:2 * MLSTM_HEADS] + br_ref[...]
    row = lax.broadcasted_iota(jnp.int32, (L, L), 0)
    col = lax.broadcasted_iota(jnp.int32, (L, L), 1)
    lower = (col <= row).astype(F32)
    upper = (row <= col).astype(F32)

    lane = lax.broadcasted_iota(jnp.int32, gc.shape, 1)
    is_f = (lane >= MLSTM_HEADS) & (lane < 2 * MLSTM_HEADS)
    lf_c = jnp.where(is_f, _log_sigmoid(gc), 0.0)
    b_c = jnp.dot(lower, lf_c, preferred_element_type=F32,
                  precision=lax.Precision.HIGHEST)
    gc_ref[...] = jnp.where(is_f, b_c, gc)

    sub = lax.broadcasted_iota(jnp.int32, gr.shape, 0)
    is_f_r = sub >= MLSTM_HEADS
    lf_r = jnp.where(is_f_r, _log_sigmoid(gr), 0.0)
    b_r = jnp.dot(lf_r, upper, preferred_element_type=F32,
                  precision=lax.Precision.HIGHEST)
    gr_ref[...] = jnp.where(is_f_r, b_r, gr)


def _gates(h, w_gate, b_gate):
    s, k = h.shape
    L = MLSTM_CHUNK
    ng = 2 * MLSTM_HEADS
    wc = jnp.zeros((k, LANES), BF16).at[:, :ng].set(w_gate.astype(BF16))
    wr = jnp.zeros((2 * SUBLANES, k), BF16).at[:ng, :].set(w_gate.T.astype(BF16))
    bc = jnp.zeros((1, LANES), F32).at[0, :ng].set(b_gate)
    br = jnp.broadcast_to(b_gate.reshape(ng, 1), (ng, L))
    return pl.pallas_call(
        _gates_kernel,
        out_shape=(jax.ShapeDtypeStruct((s, LANES), F32),
                   jax.ShapeDtypeStruct((ng, s), F32)),
        grid=(s // L,),
        in_specs=[pl.BlockSpec((L, k), lambda i: (i, 0)),
                  pl.BlockSpec((k, LANES), lambda i: (0, 0)),
                  pl.BlockSpec((2 * SUBLANES, k), lambda i: (0, 0)),
                  pl.BlockSpec((1, LANES), lambda i: (0, 0)),
                  pl.BlockSpec((ng, L), lambda i: (0, 0))],
        out_specs=(pl.BlockSpec((L, LANES), lambda i: (i, 0)),
                   pl.BlockSpec((ng, L), lambda i: (0, i))),
        compiler_params=_params("parallel"),
        name="mlstm_gates",
    )(h, wc, wr, bc, br)


def _rel_buckets_np(dist):
    n = dist.astype(np.float32)
    large = REL_MAX_EXACT + np.floor(
        np.log(np.maximum(n, 1.0) / REL_MAX_EXACT)
        / np.log(REL_MAX_DISTANCE / REL_MAX_EXACT) * (REL_BUCKETS - REL_MAX_EXACT))
    large = np.minimum(large, REL_BUCKETS - 1)
    return np.where(n < REL_MAX_EXACT, n, large).astype(np.int32)


def _attn_bias_tiles(rel_bias):
    T = ATTN_TILE
    i = np.arange(2 * T)
    rel = np.where(i <= T, -i, 2 * T - i)
    dist = np.arange(ATTN_KEY_TILES)[:, None] * T + rel[None, :]
    mult = np.zeros(dist.shape, np.int32)
    for w, d in DILATION_PATTERNS:
        mult += ((dist >= 0) & (dist <= w) & (dist % d == 0)).astype(np.int32)
    valid = mult > 0
    bucket = _rel_buckets_np(np.maximum(dist, 0))
    logmult = np.log(np.maximum(mult, 1)).astype(np.float32)
    src = rel_bias.astype(F32).T[:, bucket] + logmult[None]
    src = jnp.where(valid[None], src, MASKED)
    h = src.shape[0]
    wide = jnp.broadcast_to(src[:, :, None, :], (h, ATTN_KEY_TILES, T, 2 * T))
    flat = wide.reshape(h, ATTN_KEY_TILES, T * 2 * T)[:, :, :T * (2 * T - 1)]
    return flat.reshape(h, ATTN_KEY_TILES, T, 2 * T - 1)[:, :, :, :T]


def _attn_kernel(q_ref, k_ref, v_ref, z_ref, bias_ref, o_ref):
    T = ATTN_TILE
    qb = pl.program_id(1)
    q = (q_ref[...].astype(F32) * (ATTN_HEAD_DIM ** -0.5)).astype(BF16)

    def tile(kt, carry):
        m, l, acc = carry
        start = pl.multiple_of((qb - kt) * T, T)
        k = k_ref[pl.ds(start, T), :]
        v = v_ref[pl.ds(start, T), :]
        s = lax.dot_general(q, k, (((1,), (1,)), ((), ())),
                            preferred_element_type=F32) + bias_ref[0, kt]
        m_new = jnp.maximum(m, jnp.max(s, axis=-1, keepdims=True))
        alpha = jnp.exp(m - m_new)
        p = jnp.exp(s - m_new)
        l = alpha * l + jnp.sum(p, axis=-1, keepdims=True)
        acc = alpha * acc + jnp.dot(p.astype(BF16), v, preferred_element_type=F32)
        return m_new, l, acc

    init = (jnp.full((T, 1), MASKED, F32), jnp.zeros((T, 1), F32),
            jnp.zeros((T, ATTN_HEAD_DIM), F32))
    n_tiles = jnp.minimum(qb, ATTN_KEY_TILES - 1) + 1
    _, l, acc = lax.fori_loop(0, n_tiles, tile, init)
    o_ref[...] = (acc / l * _silu(z_ref[...].astype(F32))).astype(o_ref.dtype)


def _attention(ua, bias_tiles):
    s = ua.shape[0]
    T = ATTN_TILE
    H = ATTN_HEADS
    return pl.pallas_call(
        _attn_kernel,
        out_shape=jax.ShapeDtypeStruct((s, ATTN_WIDTH), BF16),
        grid=(H, s // T),
        in_specs=[pl.BlockSpec((T, ATTN_HEAD_DIM), lambda h, i: (i, h)),
                  pl.BlockSpec((s, ATTN_HEAD_DIM), lambda h, i: (0, H + h)),
                  pl.BlockSpec((s, ATTN_HEAD_DIM), lambda h, i: (0, 2 * H + h)),
                  pl.BlockSpec((T, ATTN_HEAD_DIM), lambda h, i: (i, 3 * H + h)),
                  pl.BlockSpec((1, ATTN_KEY_TILES, T, T), lambda h, i: (h, 0, 0, 0))],
        out_specs=pl.BlockSpec((T, ATTN_HEAD_DIM), lambda h, i: (i, h)),
        compiler_params=_params("parallel", "parallel"),
        name="dilated_attention",
    )(ua, ua, ua, ua, bias_tiles)


def _conv_silu(ext_ref, x_ref, w_ref, first):
    L = x_ref.shape[0]

    @pl.when(first)
    def _():
        ext_ref[0:CONV_HALO, :] = jnp.zeros((CONV_HALO, ext_ref.shape[1]), F32)

    ext_ref[CONV_HALO:CONV_HALO + L, :] = x_ref[...]
    acc = None
    for j in range(CONV_WIDTH):
        back = CONV_WIDTH - 1 - j
        term = ext_ref[pl.ds(CONV_HALO - back, L), :] * w_ref[j:j + 1, :]
        acc = term if acc is None else acc + term
    ext_ref[0:CONV_HALO, :] = ext_ref[L:L + CONV_HALO, :]
    return _silu(acc)


def _mlstm_kernel(q_ref, k_ref, v_ref, og_ref, zg_ref, gc_ref, gr_ref,
                  wq_ref, wk_ref, ng_ref, o_ref,
                  c_ref, n_ref, m_ref, extq_ref, extk_ref):
    L = q_ref.shape[0]
    hd = pl.program_id(0)
    first = pl.program_id(1) == 0

    @pl.when(first)
    def _():
        c_ref[...] = jnp.zeros_like(c_ref)
        n_ref[...] = jnp.zeros_like(n_ref)
        m_ref[...] = jnp.zeros_like(m_ref)

    q = _conv_silu(extq_ref, q_ref, wq_ref, first)
    k = _conv_silu(extk_ref, k_ref, wk_ref, first) * (MLSTM_QK_DIM ** -0.5)
    qb = q.astype(BF16)
    kb = k.astype(BF16)
    vb = v_ref[...].astype(BF16)

    gc = gc_ref[...]
    lane = lax.broadcasted_iota(jnp.int32, gc.shape, 1)
    li_c = jnp.sum(jnp.where(lane == hd, gc, 0.0), axis=1, keepdims=True)
    b_c = jnp.sum(jnp.where(lane == hd + MLSTM_HEADS, gc, 0.0), axis=1, keepdims=True)
    gr = gr_ref[...]
    sub = lax.broadcasted_iota(jnp.int32, gr.shape, 0)
    li_r = jnp.sum(jnp.where(sub == hd, gr, 0.0), axis=0, keepdims=True)
    b_r = jnp.sum(jnp.where(sub == hd + MLSTM_HEADS, gr, 0.0), axis=0, keepdims=True)

    m_prev = m_ref[:, 0:1]
    row = lax.broadcasted_iota(jnp.int32, (L, L), 0)
    col = lax.broadcasted_iota(jnp.int32, (L, L), 1)
    dmat = jnp.where(col <= row, b_c - b_r + li_r, MASKED)
    m_inter = b_c + m_prev
    m_t = jnp.maximum(m_inter, jnp.max(dmat, axis=1, keepdims=True))
    w_inter = jnp.exp(m_inter - m_t)
    sqk = lax.dot_general(qb, kb, (((1,), (1,)), ((), ())),
                          preferred_element_type=F32) * jnp.exp(dmat - m_t)
    num = (w_inter * jnp.dot(qb, c_ref[...].astype(BF16), preferred_element_type=F32)
           + jnp.dot(sqk.astype(BF16), vb, preferred_element_type=F32))
    nq = (w_inter * jnp.sum(q * n_ref[...], axis=1, keepdims=True)
          + jnp.sum(sqk, axis=1, keepdims=True))
    hm = num / jnp.maximum(jnp.abs(nq), jnp.exp(-m_t))

    b_last = b_c[L - 1:L, :]
    g = b_last - b_c + li_c
    m_new = jnp.maximum(b_last + m_prev, jnp.max(g, axis=0, keepdims=True))
    decay = jnp.exp(b_last + m_prev - m_new)
    kw = k * jnp.exp(g - m_new)
    c_ref[...] = decay * c_ref[...] + lax.dot_general(
        kw.astype(BF16), vb, (((0,), (0,)), ((), ())), preferred_element_type=F32)
    n_ref[...] = decay * n_ref[...] + jnp.sum(kw, axis=0, keepdims=True)
    m_ref[...] = jnp.broadcast_to(m_new, m_ref.shape)

    hn = hm * lax.rsqrt(jnp.mean(hm * hm, axis=1, keepdims=True) + NORM_EPS) * ng_ref[...]
    o_ref[...] = (hn * _sigmoid(og_ref[...]) * _silu(zg_ref[...])).astype(o_ref.dtype)


def _mlstm(um, gc, gr, conv_w, norm_g):
    s = um.shape[0]
    L = MLSTM_CHUNK
    H = MLSTM_HEADS
    dk, dv = MLSTM_QK_DIM, MLSTM_V_DIM
    kq0 = MLSTM_QK_WIDTH // dk
    v0 = 2 * MLSTM_QK_WIDTH // dv
    og0 = v0 + H
    zg0 = og0 + H
    return pl.pallas_call(
        _mlstm_kernel,
        out_shape=jax.ShapeDtypeStruct((s, MLSTM_WIDTH), BF16),
        grid=(H, s // L),
        in_specs=[pl.BlockSpec((L, dk), lambda h, c: (c, h)),
                  pl.BlockSpec((L, dk), lambda h, c: (c, kq0 + h)),
                  pl.BlockSpec((L, dv), lambda h, c: (c, v0 + h)),
                  pl.BlockSpec((L, dv), lambda h, c: (c, og0 + h)),
                  pl.BlockSpec((L, dv), lambda h, c: (c, zg0 + h)),
                  pl.BlockSpec((L, LANES), lambda h, c: (c, 0)),
                  pl.BlockSpec((2 * H, L), lambda h, c: (0, c)),
                  pl.BlockSpec((CONV_WIDTH, dk), lambda h, c: (0, h)),
                  pl.BlockSpec((CONV_WIDTH, dk), lambda h, c: (0, kq0 + h)),
                  pl.BlockSpec((1, dv), lambda h, c: (0, h))],
        out_specs=pl.BlockSpec((L, dv), lambda h, c: (c, h)),
        scratch_shapes=[pltpu.VMEM((dk, dv), F32),
                        pltpu.VMEM((1, dk), F32),
                        pltpu.VMEM((1, LANES), F32),
                        pltpu.VMEM((CONV_HALO + L, dk), F32),
                        pltpu.VMEM((CONV_HALO + L, dk), F32)],
        compiler_params=_params("parallel", "arbitrary"),
        name="mlstm",
    )(um, um, um, um, um, gc, gr, conv_w, conv_w, norm_g.reshape(1, MLSTM_WIDTH))


def _out_proj_kernel(x_ref, ya_ref, ym_ref, wa_ref, wm_ref, o_ref):
    o_ref[...] = (x_ref[...]
                  + jnp.dot(ya_ref[...], wa_ref[...], preferred_element_type=F32)
                  + jnp.dot(ym_ref[...], wm_ref[...], preferred_element_type=F32))


def _out_proj(x, ya, ym, w):
    s, d = x.shape
    ka, km = ya.shape[1], ym.shape[1]
    assert ka == km
    return pl.pallas_call(
        _out_proj_kernel,
        out_shape=jax.ShapeDtypeStruct((s, d), F32),
        grid=(s // OUT_TM, d // OUT_TN),
        in_specs=[pl.BlockSpec((OUT_TM, OUT_TN), lambda i, j: (i, j)),
                  pl.BlockSpec((OUT_TM, ka), lambda i, j: (i, 0)),
                  pl.BlockSpec((OUT_TM, km), lambda i, j: (i, 0)),
                  pl.BlockSpec((ka, OUT_TN), lambda i, j: (0, j)),
                  pl.BlockSpec((km, OUT_TN), lambda i, j: (1, j))],
        out_specs=pl.BlockSpec((OUT_TM, OUT_TN), lambda i, j: (i, j)),
        compiler_params=_params("parallel", "parallel"),
        name="out_proj",
    )(x, ya, ym, w, w)


def kernel(x, norm_g, w_in, b_gate, conv_w, mlstm_norm_g, w_out, rel_bias, final_g):
    b, s, d = x.shape
    assert b == 1
    xs = x.reshape(s, d)
    bias_tiles = _attn_bias_tiles(rel_bias)
    for l in range(DEPTH):
        h = _rmsnorm(xs, norm_g[l], BF16)
        w_l = w_in[l].astype(BF16)
        ua = _in_proj(h, w_l, 0, ATTN_COLS, BF16, "in_proj_attn")
        um = _in_proj(h, w_l, ATTN_COLS, MLSTM_COLS, F32, "in_proj_mlstm")
        gc, gr = _gates(h, w_in[l][:, GATE_COL0:], b_gate[l])
        ya = _attention(ua, bias_tiles)
        ym = _mlstm(um, gc, gr, conv_w[l], mlstm_norm_g[l])
        xs = _out_proj(xs, ya, ym, w_out[l].astype(BF16))
    return _rmsnorm(xs, final_g, F32).reshape(b, s, d)
```

```python
import math

import jax
import jax.numpy as jnp
import numpy as np
from jax import lax
from jax.experimental import pallas as pl
from jax.experimental.pallas import tpu as pltpu

D_MODEL = 4096
DEPTH = 2
ATTN_WIDTH = 2048
ATTN_HEAD_DIM = 128
ATTN_HEADS = ATTN_WIDTH // ATTN_HEAD_DIM
DILATION_PATTERNS = ((128, 1), (512, 4), (2048, 16))
REL_BUCKETS = 32
REL_MAX_EXACT = REL_BUCKETS // 2
REL_MAX_DISTANCE = 2048
MLSTM_WIDTH = 2048
MLSTM_HEADS = 4
MLSTM_V_DIM = MLSTM_WIDTH // MLSTM_HEADS
MLSTM_QK_DIM = MLSTM_V_DIM // 2
MLSTM_QK_WIDTH = MLSTM_HEADS * MLSTM_QK_DIM
CONV_WIDTH = 4
NORM_EPS = 1e-6
ATTN_COLS = 4 * ATTN_WIDTH
MLSTM_COL0 = ATTN_COLS
MLSTM_COLS = 2 * MLSTM_QK_WIDTH + 3 * MLSTM_WIDTH
GATE_COL0 = ATTN_COLS + MLSTM_COLS
NUM_GATES = 2 * MLSTM_HEADS

LANES = 128
SUBLANES = 8
VMEM_LIMIT_BYTES = 56 * 1024 * 1024

NORM_ROWS = 256
MM_TM = 1024
MM_TN = 1024
OUT_TM = 512
OUT_TN = 1024
MLSTM_CHUNK = 256
ATTN_TILE = 256
ATTN_SUB = 2
ATTN_MAX_DIST = max(w for w, _ in DILATION_PATTERNS)
ATTN_KEY_TILES = ATTN_MAX_DIST // ATTN_TILE + 1
CONV_HALO = SUBLANES
MASKED = -1e30
LOG2E = math.log2(math.e)

F32 = jnp.float32
BF16 = jnp.bfloat16
NT_DIMS = (((1,), (1,)), ((), ()))


def _params(*semantics):
    return pltpu.CompilerParams(dimension_semantics=semantics,
                                vmem_limit_bytes=VMEM_LIMIT_BYTES)


def _sigmoid(x):
    return 1.0 / (1.0 + jnp.exp(-x))


def _silu(x):
    return x * _sigmoid(x)


def _log_sigmoid(x):
    return jnp.minimum(x, 0.0) - jnp.log1p(jnp.exp(-jnp.abs(x)))


def _rmsnorm_kernel(x_ref, g_ref, o_ref):
    x = x_ref[...]
    ms = jnp.mean(x * x, axis=-1, keepdims=True)
    o_ref[...] = (x * lax.rsqrt(ms + NORM_EPS) * g_ref[...]).astype(o_ref.dtype)


def _rmsnorm(x, g, out_dtype):
    s, d = x.shape
    return pl.pallas_call(
        _rmsnorm_kernel,
        out_shape=jax.ShapeDtypeStruct((s, d), out_dtype),
        grid=(s // NORM_ROWS,),
        in_specs=[pl.BlockSpec((NORM_ROWS, d), lambda i: (i, 0)),
                  pl.BlockSpec((1, d), lambda i: (0, 0))],
        out_specs=pl.BlockSpec((NORM_ROWS, d), lambda i: (i, 0)),
        compiler_params=_params("parallel"),
        name="rmsnorm",
    )(x, g.reshape(1, d))


def _matmul_nt_kernel(a_ref, w_ref, o_ref):
    o_ref[...] = lax.dot_general(a_ref[...], w_ref[...], NT_DIMS,
                                 preferred_element_type=F32).astype(o_ref.dtype)


def _in_proj(h, wt, layer, n_cols):
    s, k = h.shape
    return pl.pallas_call(
        _matmul_nt_kernel,
        out_shape=jax.ShapeDtypeStruct((s, n_cols), BF16),
        grid=(s // MM_TM, n_cols // MM_TN),
        in_specs=[pl.BlockSpec((MM_TM, k), lambda i, j: (i, 0)),
                  pl.BlockSpec((None, MM_TN, k), lambda i, j: (layer, j, 0))],
        out_specs=pl.BlockSpec((MM_TM, MM_TN), lambda i, j: (i, j)),
        compiler_params=_params("parallel", "parallel"),
        name="in_proj",
    )(h, wt)


def _gates_kernel(h_ref, w_ref, bc_ref, br_ref, gc_ref, gr_ref):
    L = h_ref.shape[0]
    h = h_ref[...]
    w = w_ref[...]
    gc = lax.dot_general(h, w, NT_DIMS, preferred_element_type=F32) + bc_ref[...]
    gr = lax.dot_general(w[:2 * SUBLANES], h, NT_DIMS,
                         preferred_element_type=F32)[:NUM_GATES] + br_ref[...]
    row = lax.broadcasted_iota(jnp.int32, (L, L), 0)
    col = lax.broadcasted_iota(jnp.int32, (L, L), 1)
    lower = (col <= row).astype(F32)
    upper = (row <= col).astype(F32)

    lane = lax.broadcasted_iota(jnp.int32, gc.shape, 1)
    is_f = (lane >= MLSTM_HEADS) & (lane < NUM_GATES)
    lf_c = jnp.where(is_f, _log_sigmoid(gc), 0.0)
    b_c = jnp.dot(lower, lf_c, preferred_element_type=F32,
                  precision=lax.Precision.HIGHEST)
    gc_ref[...] = jnp.where(is_f, b_c, gc)

    sub = lax.broadcasted_iota(jnp.int32, gr.shape, 0)
    is_f_r = sub >= MLSTM_HEADS
    lf_r = jnp.where(is_f_r, _log_sigmoid(gr), 0.0)
    b_r = jnp.dot(lf_r, upper, preferred_element_type=F32,
                  precision=lax.Precision.HIGHEST)
    gr_ref[...] = jnp.where(is_f_r, b_r, gr)


def _gates(h, wt_gate, b_gate):
    s, k = h.shape
    L = MLSTM_CHUNK
    w = jnp.zeros((LANES, k), BF16).at[:NUM_GATES].set(wt_gate.astype(BF16))
    bc = jnp.zeros((1, LANES), F32).at[0, :NUM_GATES].set(b_gate)
    br = jnp.broadcast_to(b_gate.reshape(NUM_GATES, 1), (NUM_GATES, L))
    return pl.pallas_call(
        _gates_kernel,
        out_shape=(jax.ShapeDtypeStruct((s, LANES), F32),
                   jax.ShapeDtypeStruct((NUM_GATES, s), F32)),
        grid=(s // L,),
        in_specs=[pl.BlockSpec((L, k), lambda i: (i, 0)),
                  pl.BlockSpec((LANES, k), lambda i: (0, 0)),
                  pl.BlockSpec((1, LANES), lambda i: (0, 0)),
                  pl.BlockSpec((NUM_GATES, L), lambda i: (0, 0))],
        out_specs=(pl.BlockSpec((L, LANES), lambda i: (i, 0)),
                   pl.BlockSpec((NUM_GATES, L), lambda i: (0, i))),
        compiler_params=_params("parallel"),
        name="mlstm_gates",
    )(h, w, bc, br)


def _rel_buckets_np(dist):
    n = dist.astype(np.float32)
    large = REL_MAX_EXACT + np.floor(
        np.log(np.maximum(n, 1.0) / REL_MAX_EXACT)
        / np.log(REL_MAX_DISTANCE / REL_MAX_EXACT) * (REL_BUCKETS - REL_MAX_EXACT))
    large = np.minimum(large, REL_BUCKETS - 1)
    return np.where(n < REL_MAX_EXACT, n, large).astype(np.int32)


def _attn_bias_rows(rel_bias):
    T = ATTN_TILE
    i = np.arange(2 * T)
    rel = np.where(i <= T, -i, 2 * T - i)
    dist = np.arange(ATTN_KEY_TILES + 1)[:, None] * T + rel[None, :]
    mult = np.zeros(dist.shape, np.int32)
    for w, d in DILATION_PATTERNS:
        mult += ((dist >= 0) & (dist <= w) & (dist % d == 0)).astype(np.int32)
    bucket = _rel_buckets_np(np.maximum(dist, 0))
    logmult = np.log(np.maximum(mult, 1)).astype(np.float32)
    rows = (rel_bias.astype(F32).T[:, bucket] + logmult[None]) * LOG2E
    return jnp.where((mult > 0)[None], rows, MASKED)


def _bias_tiles_kernel(rows_ref, o_ref):
    T = ATTN_TILE
    for kt in range(ATTN_KEY_TILES + 1):
        wide = jnp.broadcast_to(rows_ref[0, kt:kt + 1, :], (T, 2 * T))
        o_ref[0, kt] = pltpu.roll(wide, 0, 1, stride=1, stride_axis=0)[:, :T]


def _attn_bias_tiles(rel_bias):
    T = ATTN_TILE
    rows = _attn_bias_rows(rel_bias)
    h, nt, _ = rows.shape
    return pl.pallas_call(
        _bias_tiles_kernel,
        out_shape=jax.ShapeDtypeStruct((h, nt, T, T), F32),
        grid=(h,),
        in_specs=[pl.BlockSpec((1, nt, 2 * T), lambda i: (i, 0, 0))],
        out_specs=pl.BlockSpec((1, nt, T, T), lambda i: (i, 0, 0, 0)),
        compiler_params=_params("parallel"),
        name="attn_bias_tiles",
    )(rows)


def _attn_kernel(q_ref, k_ref, v_ref, z_ref, bias_ref, o_ref, s_ref):
    T = ATTN_TILE
    D = ATTN_HEAD_DIM
    step = pl.program_id(1)
    ones = jnp.ones((T, D), BF16)
    for u in range(ATTN_SUB):
        rows = slice(u * T, (u + 1) * T)
        j = step * ATTN_SUB + u
        q = (q_ref[rows, :].astype(F32) * (D ** -0.5 * LOG2E)).astype(BF16)

        def key_start(kt):
            return pl.multiple_of(jnp.maximum(j - kt, 0) * T, T)

        m = None
        for kt in range(ATTN_KEY_TILES):
            bias_idx = jnp.where(kt <= j, kt, ATTN_KEY_TILES)
            k = k_ref[pl.ds(key_start(kt), T), :]
            s = lax.dot_general(q, k, NT_DIMS, preferred_element_type=F32) + bias_ref[0, bias_idx]
            s_ref[u, kt] = s
            mt = jnp.max(s, axis=-1, keepdims=True)
            m = mt if m is None else jnp.maximum(m, mt)

        acc = jnp.zeros((T, 2 * D), F32)
        for kt in range(ATTN_KEY_TILES):
            p = jnp.exp2(s_ref[u, kt] - m).astype(BF16)
            v = v_ref[pl.ds(key_start(kt), T), :]
            acc = acc + jnp.dot(p, jnp.concatenate([v, ones], axis=1),
                                preferred_element_type=F32)
        o = acc[:, :D] / acc[:, D:]
        o_ref[rows, :] = (o * _silu(z_ref[rows, :].astype(F32))).astype(o_ref.dtype)


def _attention(u, bias_tiles):
    s = u.shape[0]
    T = ATTN_TILE
    R = ATTN_SUB * T
    H = ATTN_HEADS
    D = ATTN_HEAD_DIM
    nt = ATTN_KEY_TILES + 1
    return pl.pallas_call(
        _attn_kernel,
        out_shape=jax.ShapeDtypeStruct((s, ATTN_WIDTH), BF16),
        grid=(H, s // R),
        in_specs=[pl.BlockSpec((R, D), lambda h, i: (i, h)),
                  pl.BlockSpec((s, D), lambda h, i: (0, H + h)),
                  pl.BlockSpec((s, D), lambda h, i: (0, 2 * H + h)),
                  pl.BlockSpec((R, D), lambda h, i: (i, 3 * H + h)),
                  pl.BlockSpec((1, nt, T, T), lambda h, i: (h, 0, 0, 0))],
        out_specs=pl.BlockSpec((R, D), lambda h, i: (i, h)),
        scratch_shapes=[pltpu.VMEM((ATTN_SUB, ATTN_KEY_TILES, T, T), F32)],
        compiler_params=_params("parallel", "parallel"),
        name="dilated_attention",
    )(u, u, u, u, bias_tiles)


def _conv_silu(ext_ref, x_ref, w_ref, first):
    L = x_ref.shape[0]

    @pl.when(first)
    def _():
        ext_ref[0:CONV_HALO, :] = jnp.zeros((CONV_HALO, ext_ref.shape[1]), F32)

    ext_ref[CONV_HALO:CONV_HALO + L, :] = x_ref[...].astype(F32)
    acc = None
    for j in range(CONV_WIDTH):
        back = CONV_WIDTH - 1 - j
        term = ext_ref[pl.ds(CONV_HALO - back, L), :] * w_ref[j:j + 1, :]
        acc = term if acc is None else acc + term
    ext_ref[0:CONV_HALO, :] = ext_ref[L:L + CONV_HALO, :]
    return _silu(acc)


def _mlstm_kernel(q_ref, k_ref, v_ref, og_ref, zg_ref, gc_ref, gr_ref,
                  wq_ref, wk_ref, ng_ref, o_ref,
                  c_ref, n_ref, m_ref, extq_ref, extk_ref):
    L = q_ref.shape[0]
    hd = pl.program_id(0)
    first = pl.program_id(1) == 0

    @pl.when(first)
    def _():
        c_ref[...] = jnp.zeros_like(c_ref)
        n_ref[...] = jnp.zeros_like(n_ref)
        m_ref[...] = jnp.zeros_like(m_ref)

    q = _conv_silu(extq_ref, q_ref, wq_ref, first)
    k = _conv_silu(extk_ref, k_ref, wk_ref, first) * (MLSTM_QK_DIM ** -0.5)
    qb = q.astype(BF16)
    kb = k.astype(BF16)
    vb = v_ref[...]

    gc = gc_ref[...]
    lane = lax.broadcasted_iota(jnp.int32, gc.shape, 1)
    li_c = jnp.sum(jnp.where(lane == hd, gc, 0.0), axis=1, keepdims=True)
    b_c = jnp.sum(jnp.where(lane == hd + MLSTM_HEADS, gc, 0.0), axis=1, keepdims=True)
    gr = gr_ref[...]
    sub = lax.broadcasted_iota(jnp.int32, gr.shape, 0)
    li_r = jnp.sum(jnp.where(sub == hd, gr, 0.0), axis=0, keepdims=True)
    b_r = jnp.sum(jnp.where(sub == hd + MLSTM_HEADS, gr, 0.0), axis=0, keepdims=True)

    m_prev = m_ref[:, 0:1]
    row = lax.broadcasted_iota(jnp.int32, (L, L), 0)
    col = lax.broadcasted_iota(jnp.int32, (L, L), 1)
    dmat = jnp.where(col <= row, b_c - b_r + li_r, MASKED)
    m_inter = b_c + m_prev
    m_t = jnp.maximum(m_inter, jnp.max(dmat, axis=1, keepdims=True))
    w_inter = jnp.exp(m_inter - m_t)
    sqk = lax.dot_general(qb, kb, NT_DIMS, preferred_element_type=F32) * jnp.exp(dmat - m_t)
    num = (w_inter * jnp.dot(qb, c_ref[...].astype(BF16), preferred_element_type=F32)
           + jnp.dot(sqk.astype(BF16), vb, preferred_element_type=F32))
    nq = (w_inter * jnp.sum(q * n_ref[...], axis=1, keepdims=True)
          + jnp.sum(sqk, axis=1, keepdims=True))
    hm = num / jnp.maximum(jnp.abs(nq), jnp.exp(-m_t))

    b_last = b_c[L - 1:L, :]
    g = b_last - b_c + li_c
    m_new = jnp.maximum(b_last + m_prev, jnp.max(g, axis=0, keepdims=True))
    decay = jnp.exp(b_last + m_prev - m_new)
    kw = k * jnp.exp(g - m_new)
    c_ref[...] = decay * c_ref[...] + lax.dot_general(
        kw.astype(BF16), vb, (((0,), (0,)), ((), ())), preferred_element_type=F32)
    n_ref[...] = decay * n_ref[...] + jnp.sum(kw, axis=0, keepdims=True)
    m_ref[...] = jnp.broadcast_to(m_new, m_ref.shape)

    hn = hm * lax.rsqrt(jnp.mean(hm * hm, axis=1, keepdims=True) + NORM_EPS) * ng_ref[...]
    o_ref[...] = (hn * _sigmoid(og_ref[...].astype(F32))
                  * _silu(zg_ref[...].astype(F32))).astype(o_ref.dtype)


def _mlstm(u, gc, gr, conv_w, norm_g):
    s = u.shape[0]
    L = MLSTM_CHUNK
    H = MLSTM_HEADS
    dk, dv = MLSTM_QK_DIM, MLSTM_V_DIM
    q0 = MLSTM_COL0 // dk
    k0 = q0 + H
    v0 = (MLSTM_COL0 + 2 * MLSTM_QK_WIDTH) // dv
    og0 = v0 + H
    zg0 = og0 + H
    return pl.pallas_call(
        _mlstm_kernel,
        out_shape=jax.ShapeDtypeStruct((s, MLSTM_WIDTH), BF16),
        grid=(H, s // L),
        in_specs=[pl.BlockSpec((L, dk), lambda h, c: (c, q0 + h)),
                  pl.BlockSpec((L, dk), lambda h, c: (c, k0 + h)),
                  pl.BlockSpec((L, dv), lambda h, c: (c, v0 + h)),
                  pl.BlockSpec((L, dv), lambda h, c: (c, og0 + h)),
                  pl.BlockSpec((L, dv), lambda h, c: (c, zg0 + h)),
                  pl.BlockSpec((L, LANES), lambda h, c: (c, 0)),
                  pl.BlockSpec((NUM_GATES, L), lambda h, c: (0, c)),
                  pl.BlockSpec((CONV_WIDTH, dk), lambda h, c: (0, h)),
                  pl.BlockSpec((CONV_WIDTH, dk), lambda h, c: (0, H + h)),
                  pl.BlockSpec((1, dv), lambda h, c: (0, h))],
        out_specs=pl.BlockSpec((L, dv), lambda h, c: (c, h)),
        scratch_shapes=[pltpu.VMEM((dk, dv), F32),
                        pltpu.VMEM((1, dk), F32),
                        pltpu.VMEM((1, LANES), F32),
                        pltpu.VMEM((CONV_HALO + L, dk), F32),
                        pltpu.VMEM((CONV_HALO + L, dk), F32)],
        compiler_params=_params("parallel", "arbitrary"),
        name="mlstm",
    )(u, u, u, u, u, gc, gr, conv_w, conv_w, norm_g.reshape(1, MLSTM_WIDTH))


def _out_proj_kernel(x_ref, ya_ref, ym_ref, wa_ref, wm_ref, o_ref):
    o_ref[...] = (x_ref[...]
                  + jnp.dot(ya_ref[...], wa_ref[...], preferred_element_type=F32)
                  + jnp.dot(ym_ref[...], wm_ref[...], preferred_element_type=F32))


def _out_proj(x, ya, ym, w, layer):
    s, d = x.shape
    ka, km = ya.shape[1], ym.shape[1]
    assert ka == km
    return pl.pallas_call(
        _out_proj_kernel,
        out_shape=jax.ShapeDtypeStruct((s, d), F32),
        grid=(s // OUT_TM, d // OUT_TN),
        in_specs=[pl.BlockSpec((OUT_TM, OUT_TN), lambda i, j: (i, j)),
                  pl.BlockSpec((OUT_TM, ka), lambda i, j: (i, 0)),
                  pl.BlockSpec((OUT_TM, km), lambda i, j: (i, 0)),
                  pl.BlockSpec((None, ka, OUT_TN), lambda i, j: (layer, 0, j)),
                  pl.BlockSpec((None, km, OUT_TN), lambda i, j: (layer, 1, j))],
        out_specs=pl.BlockSpec((OUT_TM, OUT_TN), lambda i, j: (i, j)),
        compiler_params=_params("parallel", "parallel"),
        name="out_proj",
    )(x, ya, ym, w, w)


def kernel(x, norm_g, w_in, b_gate, conv_w, mlstm_norm_g, w_out, rel_bias, final_g):
    b, s, d = x.shape
    assert b == 1
    xs = x.reshape(s, d)
    bias_tiles = _attn_bias_tiles(rel_bias)
    wt_in = jnp.swapaxes(w_in, 1, 2).astype(BF16)
    w_out_b = w_out.astype(BF16)
    for l in range(DEPTH):
        h = _rmsnorm(xs, norm_g[l], BF16)
        u = _in_proj(h, wt_in, l, GATE_COL0)
        gc, gr = _gates(h, wt_in[l, GATE_COL0:], b_gate[l])
        ya = _attention(u, bias_tiles)
        ym = _mlstm(u, gc, gr, conv_w[l], mlstm_norm_g[l])
        xs = _out_proj(xs, ya, ym, w_out_b, l)
    return _rmsnorm(xs, final_g, F32).reshape(b, s, d)
```

```python
import math

import jax
import jax.numpy as jnp
import numpy as np
from jax import lax
from jax.experimental import pallas as pl
from jax.experimental.pallas import tpu as pltpu

D_MODEL = 4096
DEPTH = 2
ATTN_WIDTH = 2048
ATTN_HEAD_DIM = 128
ATTN_HEADS = ATTN_WIDTH // ATTN_HEAD_DIM
DILATION_PATTERNS = ((128, 1), (512, 4), (2048, 16))
REL_BUCKETS = 32
REL_MAX_EXACT = REL_BUCKETS // 2
REL_MAX_DISTANCE = 2048
MLSTM_WIDTH = 2048
MLSTM_HEADS = 4
MLSTM_V_DIM = MLSTM_WIDTH // MLSTM_HEADS
MLSTM_QK_DIM = MLSTM_V_DIM // 2
MLSTM_QK_WIDTH = MLSTM_HEADS * MLSTM_QK_DIM
CONV_WIDTH = 4
NORM_EPS = 1e-6
ATTN_COLS = 4 * ATTN_WIDTH
MLSTM_COL0 = ATTN_COLS
MLSTM_COLS = 2 * MLSTM_QK_WIDTH + 3 * MLSTM_WIDTH
GATE_COL0 = ATTN_COLS + MLSTM_COLS
NUM_GATES = 2 * MLSTM_HEADS

LANES = 128
SUBLANES = 8
VMEM_LIMIT_BYTES = 56 * 1024 * 1024

NORM_ROWS = 256
MM_TM = 2048
MM_TN = 1024
OUT_TM = 1024
OUT_TN = 1024
MLSTM_CHUNK = 256
ATTN_TILE = 256
ATTN_SUB = 2
ATTN_MAX_DIST = max(w for w, _ in DILATION_PATTERNS)
ATTN_KEY_TILES = ATTN_MAX_DIST // ATTN_TILE + 1
CONV_HALO = SUBLANES
MASKED = -1e30
LOG2E = math.log2(math.e)

F32 = jnp.float32
BF16 = jnp.bfloat16
NT_DIMS = (((1,), (1,)), ((), ()))
RESIDENT = pl.Buffered(1)


def _params(*semantics):
    return pltpu.CompilerParams(dimension_semantics=semantics,
                                vmem_limit_bytes=VMEM_LIMIT_BYTES)


def _sigmoid(x):
    return 1.0 / (1.0 + jnp.exp2(x * (-LOG2E)))


def _silu(x):
    return x * _sigmoid(x)


def _log_sigmoid(x):
    return jnp.minimum(x, 0.0) - jnp.log1p(jnp.exp(-jnp.abs(x)))


def _rmsnorm_kernel(x_ref, g_ref, o_ref):
    x = x_ref[...]
    ms = jnp.mean(x * x, axis=-1, keepdims=True)
    o_ref[...] = (x * lax.rsqrt(ms + NORM_EPS) * g_ref[...]).astype(o_ref.dtype)


def _rmsnorm(x, g, out_dtype):
    s, d = x.shape
    return pl.pallas_call(
        _rmsnorm_kernel,
        out_shape=jax.ShapeDtypeStruct((s, d), out_dtype),
        grid=(s // NORM_ROWS,),
        in_specs=[pl.BlockSpec((NORM_ROWS, d), lambda i: (i, 0)),
                  pl.BlockSpec((1, d), lambda i: (0, 0))],
        out_specs=pl.BlockSpec((NORM_ROWS, d), lambda i: (i, 0)),
        compiler_params=_params("parallel"),
        name="rmsnorm",
    )(x, g.reshape(1, d))


def _matmul_nt_kernel(a_ref, w_ref, o_ref):
    half = a_ref.shape[0] // 2
    for r in range(2):
        rows = slice(r * half, (r + 1) * half)
        o_ref[rows, :] = lax.dot_general(a_ref[rows, :], w_ref[...], NT_DIMS,
                                         preferred_element_type=F32).astype(o_ref.dtype)


def _in_proj(h, wt, layer, n_cols):
    s, k = h.shape
    return pl.pallas_call(
        _matmul_nt_kernel,
        out_shape=jax.ShapeDtypeStruct((s, n_cols), BF16),
        grid=(s // MM_TM, n_cols // MM_TN),
        in_specs=[pl.BlockSpec((MM_TM, k), lambda i, j: (i, 0), pipeline_mode=RESIDENT),
                  pl.BlockSpec((None, MM_TN, k), lambda i, j: (layer, j, 0))],
        out_specs=pl.BlockSpec((MM_TM, MM_TN), lambda i, j: (i, j)),
        compiler_params=_params("parallel", "parallel"),
        name="in_proj",
    )(h, wt)


def _gates_kernel(h_ref, w_ref, bc_ref, br_ref, gc_ref, gr_ref):
    L = h_ref.shape[0]
    h = h_ref[...]
    w = w_ref[...]
    gc = lax.dot_general(h, w, NT_DIMS, preferred_element_type=F32) + bc_ref[...]
    gr = lax.dot_general(w[:2 * SUBLANES], h, NT_DIMS,
                         preferred_element_type=F32)[:NUM_GATES] + br_ref[...]
    row = lax.broadcasted_iota(jnp.int32, (L, L), 0)
    col = lax.broadcasted_iota(jnp.int32, (L, L), 1)
    lower = (col <= row).astype(F32)
    upper = (row <= col).astype(F32)

    lane = lax.broadcasted_iota(jnp.int32, gc.shape, 1)
    is_f = (lane >= MLSTM_HEADS) & (lane < NUM_GATES)
    lf_c = jnp.where(is_f, _log_sigmoid(gc), 0.0)
    b_c = jnp.dot(lower, lf_c, preferred_element_type=F32,
                  precision=lax.Precision.HIGHEST)
    gc_ref[...] = jnp.where(is_f, b_c, gc)

    sub = lax.broadcasted_iota(jnp.int32, gr.shape, 0)
    is_f_r = sub >= MLSTM_HEADS
    lf_r = jnp.where(is_f_r, _log_sigmoid(gr), 0.0)
    b_r = jnp.dot(lf_r, upper, preferred_element_type=F32,
                  precision=lax.Precision.HIGHEST)
    gr_ref[...] = jnp.where(is_f_r, b_r, gr)


def _gates(h, wt_gate, b_gate):
    s, k = h.shape
    L = MLSTM_CHUNK
    w = jnp.zeros((LANES, k), BF16).at[:NUM_GATES].set(wt_gate.astype(BF16))
    bc = jnp.zeros((1, LANES), F32).at[0, :NUM_GATES].set(b_gate)
    br = jnp.broadcast_to(b_gate.reshape(NUM_GATES, 1), (NUM_GATES, L))
    return pl.pallas_call(
        _gates_kernel,
        out_shape=(jax.ShapeDtypeStruct((s, LANES), F32),
                   jax.ShapeDtypeStruct((NUM_GATES, s), F32)),
        grid=(s // L,),
        in_specs=[pl.BlockSpec((L, k), lambda i: (i, 0)),
                  pl.BlockSpec((LANES, k), lambda i: (0, 0)),
                  pl.BlockSpec((1, LANES), lambda i: (0, 0)),
                  pl.BlockSpec((NUM_GATES, L), lambda i: (0, 0))],
        out_specs=(pl.BlockSpec((L, LANES), lambda i: (i, 0)),
                   pl.BlockSpec((NUM_GATES, L), lambda i: (0, i))),
        compiler_params=_params("parallel"),
        name="mlstm_gates",
    )(h, w, bc, br)


def _rel_buckets_np(dist):
    n = dist.astype(np.float32)
    large = REL_MAX_EXACT + np.floor(
        np.log(np.maximum(n, 1.0) / REL_MAX_EXACT)
        / np.log(REL_MAX_DISTANCE / REL_MAX_EXACT) * (REL_BUCKETS - REL_MAX_EXACT))
    large = np.minimum(large, REL_BUCKETS - 1)
    return np.where(n < REL_MAX_EXACT, n, large).astype(np.int32)


def _attn_bias_rows(rel_bias):
    T = ATTN_TILE
    i = np.arange(2 * T)
    rel = np.where(i <= T, -i, 2 * T - i)
    dist = np.arange(ATTN_KEY_TILES + 1)[:, None] * T + rel[None, :]
    mult = np.zeros(dist.shape, np.int32)
    for w, d in DILATION_PATTERNS:
        mult += ((dist >= 0) & (dist <= w) & (dist % d == 0)).astype(np.int32)
    bucket = _rel_buckets_np(np.maximum(dist, 0))
    logmult = np.log(np.maximum(mult, 1)).astype(np.float32)
    rows = (rel_bias.astype(F32).T[:, bucket] + logmult[None]) * LOG2E
    return jnp.where((mult > 0)[None], rows, MASKED)


def _bias_tiles_kernel(rows_ref, o_ref):
    T = ATTN_TILE
    for kt in range(ATTN_KEY_TILES + 1):
        wide = jnp.broadcast_to(rows_ref[0, kt:kt + 1, :], (T, 2 * T))
        o_ref[0, kt] = pltpu.roll(wide, 0, 1, stride=1, stride_axis=0)[:, :T]


def _attn_bias_tiles(rel_bias):
    T = ATTN_TILE
    rows = _attn_bias_rows(rel_bias)
    h, nt, _ = rows.shape
    return pl.pallas_call(
        _bias_tiles_kernel,
        out_shape=jax.ShapeDtypeStruct((h, nt, T, T), F32),
        grid=(h,),
        in_specs=[pl.BlockSpec((1, nt, 2 * T), lambda i: (i, 0, 0))],
        out_specs=pl.BlockSpec((1, nt, T, T), lambda i: (i, 0, 0, 0)),
        compiler_params=_params("parallel"),
        name="attn_bias_tiles",
    )(rows)


def _attn_kernel(q_ref, k_ref, v_ref, z_ref, bias_ref, o_ref, s_ref):
    T = ATTN_TILE
    D = ATTN_HEAD_DIM
    step = pl.program_id(1)
    ones = jnp.ones((T, D), BF16)
    for u in range(ATTN_SUB):
        rows = slice(u * T, (u + 1) * T)
        j = step * ATTN_SUB + u
        q = (q_ref[rows, :].astype(F32) * (D ** -0.5 * LOG2E)).astype(BF16)

        def key_start(kt):
            return pl.multiple_of(jnp.maximum(j - kt, 0) * T, T)

        m = None
        for kt in range(ATTN_KEY_TILES):
            bias_idx = jnp.where(kt <= j, kt, ATTN_KEY_TILES)
            k = k_ref[pl.ds(key_start(kt), T), :]
            s = lax.dot_general(q, k, NT_DIMS, preferred_element_type=F32) + bias_ref[0, bias_idx]
            s_ref[u, kt] = s
            mt = jnp.max(s, axis=-1, keepdims=True)
            m = mt if m is None else jnp.maximum(m, mt)

        acc = jnp.zeros((T, 2 * D), F32)
        for kt in range(ATTN_KEY_TILES):
            p = jnp.exp2(s_ref[u, kt] - m).astype(BF16)
            v = v_ref[pl.ds(key_start(kt), T), :]
            acc = acc + jnp.dot(p, jnp.concatenate([v, ones], axis=1),
                                preferred_element_type=F32)
        o = acc[:, :D] / acc[:, D:]
        o_ref[rows, :] = (o * _silu(z_ref[rows, :].astype(F32))).astype(o_ref.dtype)


def _attention(u, bias_tiles):
    s = u.shape[0]
    T = ATTN_TILE
    R = ATTN_SUB * T
    H = ATTN_HEADS
    D = ATTN_HEAD_DIM
    nt = ATTN_KEY_TILES + 1
    return pl.pallas_call(
        _attn_kernel,
        out_shape=jax.ShapeDtypeStruct((s, ATTN_WIDTH), BF16),
        grid=(H, s // R),
        in_specs=[pl.BlockSpec((R, D), lambda h, i: (i, h)),
                  pl.BlockSpec((s, D), lambda h, i: (0, H + h)),
                  pl.BlockSpec((s, D), lambda h, i: (0, 2 * H + h)),
                  pl.BlockSpec((R, D), lambda h, i: (i, 3 * H + h)),
                  pl.BlockSpec((1, nt, T, T), lambda h, i: (h, 0, 0, 0))],
        out_specs=pl.BlockSpec((R, D), lambda h, i: (i, h)),
        scratch_shapes=[pltpu.VMEM((ATTN_SUB, ATTN_KEY_TILES, T, T), F32)],
        compiler_params=_params("parallel", "parallel"),
        name="dilated_attention",
    )(u, u, u, u, bias_tiles)


def _conv_silu(ext_ref, x_ref, w_ref, cols):
    L = x_ref.shape[0]
    ext_ref[CONV_HALO:CONV_HALO + L, cols] = x_ref[:, cols].astype(F32)
    acc = None
    for j in range(CONV_WIDTH):
        back = CONV_WIDTH - 1 - j
        term = ext_ref[pl.ds(CONV_HALO - back, L), cols] * w_ref[j:j + 1, cols]
        acc = term if acc is None else acc + term
    ext_ref[0:CONV_HALO, cols] = ext_ref[L:L + CONV_HALO, cols]
    return _silu(acc)


def _mlstm_head(hd, q_ref, k_ref, v_ref, og_ref, zg_ref, gc, gr, wq_ref, wk_ref,
                ng_ref, o_ref, c_ref, n_ref, m_ref, extq_ref, extk_ref, causal):
    L = q_ref.shape[0]
    dk, dv = MLSTM_QK_DIM, MLSTM_V_DIM
    kcols = slice(hd * dk, (hd + 1) * dk)
    vcols = slice(hd * dv, (hd + 1) * dv)

    q = _conv_silu(extq_ref, q_ref, wq_ref, kcols)
    k = _conv_silu(extk_ref, k_ref, wk_ref, kcols) * (dk ** -0.5)
    qb = q.astype(BF16)
    kb = k.astype(BF16)
    vb = v_ref[:, vcols]

    li_c = gc[:, hd:hd + 1]
    b_c = gc[:, MLSTM_HEADS + hd:MLSTM_HEADS + hd + 1]
    li_r = gr[hd:hd + 1, :]
    b_r = gr[MLSTM_HEADS + hd:MLSTM_HEADS + hd + 1, :]

    m_prev = m_ref[hd, :, 0:1]
    dmat = jnp.where(causal, b_c - b_r + li_r, MASKED)
    m_inter = b_c + m_prev
    m_t = jnp.maximum(m_inter, jnp.max(dmat, axis=1, keepdims=True))
    w_inter = jnp.exp(m_inter - m_t)
    sqk = lax.dot_general(qb, kb, NT_DIMS, preferred_element_type=F32) * jnp.exp(dmat - m_t)
    num = (w_inter * jnp.dot(qb, c_ref[hd].astype(BF16), preferred_element_type=F32)
           + jnp.dot(sqk.astype(BF16), vb, preferred_element_type=F32))
    nq = (w_inter * jnp.sum(q * n_ref[hd], axis=1, keepdims=True)
          + jnp.sum(sqk, axis=1, keepdims=True))
    rden = 1.0 / jnp.maximum(jnp.abs(nq), jnp.exp(-m_t))
    ms = jnp.mean(num * num, axis=1, keepdims=True)
    row_scale = rden * lax.rsqrt(ms * rden * rden + NORM_EPS)
    gate = _sigmoid(og_ref[:, vcols].astype(F32)) * _silu(zg_ref[:, vcols].astype(F32))
    o_ref[:, vcols] = (num * row_scale * ng_ref[:, vcols] * gate).astype(o_ref.dtype)

    b_last = b_c[L - 1:L, :]
    g = b_last - b_c + li_c
    m_new = jnp.maximum(b_last + m_prev, jnp.max(g, axis=0, keepdims=True))
    decay = jnp.exp(b_last + m_prev - m_new)
    kw = k * jnp.exp(g - m_new)
    c_ref[hd] = decay * c_ref[hd] + lax.dot_general(
        kw.astype(BF16), vb, (((0,), (0,)), ((), ())), preferred_element_type=F32)
    n_ref[hd] = decay * n_ref[hd] + jnp.sum(kw, axis=0, keepdims=True)
    m_ref[hd] = jnp.broadcast_to(m_new, m_ref.shape[1:])


def _mlstm_kernel(q_ref, k_ref, v_ref, og_ref, zg_ref, gc_ref, gr_ref,
                  wq_ref, wk_ref, ng_ref, o_ref,
                  c_ref, n_ref, m_ref, extq_ref, extk_ref):
    L = q_ref.shape[0]

    @pl.when(pl.program_id(0) == 0)
    def _():
        c_ref[...] = jnp.zeros_like(c_ref)
        n_ref[...] = jnp.zeros_like(n_ref)
        m_ref[...] = jnp.zeros_like(m_ref)
        extq_ref[0:CONV_HALO, :] = jnp.zeros((CONV_HALO, extq_ref.shape[1]), F32)
        extk_ref[0:CONV_HALO, :] = jnp.zeros((CONV_HALO, extk_ref.shape[1]), F32)

    gc = gc_ref[...]
    gr = gr_ref[...]
    row = lax.broadcasted_iota(jnp.int32, (L, L), 0)
    col = lax.broadcasted_iota(jnp.int32, (L, L), 1)
    causal = col <= row
    for hd in range(MLSTM_HEADS):
        _mlstm_head(hd, q_ref, k_ref, v_ref, og_ref, zg_ref, gc, gr, wq_ref, wk_ref,
                    ng_ref, o_ref, c_ref, n_ref, m_ref, extq_ref, extk_ref, causal)


def _mlstm(u, gc, gr, conv_w, norm_g):
    s = u.shape[0]
    L = MLSTM_CHUNK
    H = MLSTM_HEADS
    dk, dv = MLSTM_QK_DIM, MLSTM_V_DIM
    qkw, vw = MLSTM_QK_WIDTH, MLSTM_WIDTH
    q0 = MLSTM_COL0 // qkw
    k0 = q0 + 1
    v0 = (MLSTM_COL0 + 2 * qkw) // vw
    og0 = v0 + 1
    zg0 = og0 + 1
    return pl.pallas_call(
        _mlstm_kernel,
        out_shape=jax.ShapeDtypeStruct((s, vw), BF16),
        grid=(s // L,),
        in_specs=[pl.BlockSpec((L, qkw), lambda c: (c, q0)),
                  pl.BlockSpec((L, qkw), lambda c: (c, k0)),
                  pl.BlockSpec((L, vw), lambda c: (c, v0)),
                  pl.BlockSpec((L, vw), lambda c: (c, og0)),
                  pl.BlockSpec((L, vw), lambda c: (c, zg0)),
                  pl.BlockSpec((L, LANES), lambda c: (c, 0)),
                  pl.BlockSpec((NUM_GATES, L), lambda c: (0, c)),
                  pl.BlockSpec((CONV_WIDTH, qkw), lambda c: (0, 0)),
                  pl.BlockSpec((CONV_WIDTH, qkw), lambda c: (0, 1)),
                  pl.BlockSpec((1, vw), lambda c: (0, 0))],
        out_specs=pl.BlockSpec((L, vw), lambda c: (c, 0)),
        scratch_shapes=[pltpu.VMEM((H, dk, dv), F32),
                        pltpu.VMEM((H, 1, dk), F32),
                        pltpu.VMEM((H, 1, LANES), F32),
                        pltpu.VMEM((CONV_HALO + L, qkw), F32),
                        pltpu.VMEM((CONV_HALO + L, qkw), F32)],
        compiler_params=_params("arbitrary"),
        name="mlstm",
    )(u, u, u, u, u, gc, gr, conv_w, conv_w, norm_g.reshape(1, vw))


def _out_proj_kernel(x_ref, ya_ref, ym_ref, wa_ref, wm_ref, o_ref):
    o_ref[...] = (x_ref[...]
                  + jnp.dot(ya_ref[...], wa_ref[...], preferred_element_type=F32)
                  + jnp.dot(ym_ref[...], wm_ref[...], preferred_element_type=F32))


def _out_proj(x, ya, ym, w, layer):
    s, d = x.shape
    ka, km = ya.shape[1], ym.shape[1]
    assert ka == km
    return pl.pallas_call(
        _out_proj_kernel,
        out_shape=jax.ShapeDtypeStruct((s, d), F32),
        grid=(s // OUT_TM, d // OUT_TN),
        in_specs=[pl.BlockSpec((OUT_TM, OUT_TN), lambda i, j: (i, j)),
                  pl.BlockSpec((OUT_TM, ka), lambda i, j: (i, 0), pipeline_mode=RESIDENT),
                  pl.BlockSpec((OUT_TM, km), lambda i, j: (i, 0), pipeline_mode=RESIDENT),
                  pl.BlockSpec((None, ka, OUT_TN), lambda i, j: (layer, 0, j)),
                  pl.BlockSpec((None, km, OUT_TN), lambda i, j: (layer, 1, j))],
        out_specs=pl.BlockSpec((OUT_TM, OUT_TN), lambda i, j: (i, j)),
        compiler_params=_params("parallel", "parallel"),
        name="out_proj",
    )(x, ya, ym, w, w)


def kernel(x, norm_g, w_in, b_gate, conv_w, mlstm_norm_g, w_out, rel_bias, final_g):
    b, s, d = x.shape
    assert b == 1
    xs = x.reshape(s, d)
    bias_tiles = _attn_bias_tiles(rel_bias)
    wt_in = jnp.swapaxes(w_in, 1, 2).astype(BF16)
    w_out_b = w_out.astype(BF16)
    for l in range(DEPTH):
        h = _rmsnorm(xs, norm_g[l], BF16)
        u = _in_proj(h, wt_in, l, GATE_COL0)
        gc, gr = _gates(h, wt_in[l, GATE_COL0:], b_gate[l])
        ya = _attention(u, bias_tiles)
        ym = _mlstm(u, gc, gr, conv_w[l], mlstm_norm_g[l])
        xs = _out_proj(xs, ya, ym, w_out_b, l)
    return _rmsnorm(xs, final_g, F32).reshape(b, s, d)
```

```python
import math

import jax
import jax.numpy as jnp
import numpy as np
from jax import lax
from jax.experimental import pallas as pl
from jax.experimental.pallas import tpu as pltpu

D_MODEL = 4096
DEPTH = 2
ATTN_WIDTH = 2048
ATTN_HEAD_DIM = 128
ATTN_HEADS = ATTN_WIDTH // ATTN_HEAD_DIM
DILATION_PATTERNS = ((128, 1), (512, 4), (2048, 16))
REL_BUCKETS = 32
REL_MAX_EXACT = REL_BUCKETS // 2
REL_MAX_DISTANCE = 2048
MLSTM_WIDTH = 2048
MLSTM_HEADS = 4
MLSTM_V_DIM = MLSTM_WIDTH // MLSTM_HEADS
MLSTM_QK_DIM = MLSTM_V_DIM // 2
MLSTM_QK_WIDTH = MLSTM_HEADS * MLSTM_QK_DIM
CONV_WIDTH = 4
NORM_EPS = 1e-6
ATTN_COLS = 4 * ATTN_WIDTH
MLSTM_COL0 = ATTN_COLS
MLSTM_COLS = 2 * MLSTM_QK_WIDTH + 3 * MLSTM_WIDTH
GATE_COL0 = ATTN_COLS + MLSTM_COLS
NUM_GATES = 2 * MLSTM_HEADS

LANES = 128
SUBLANES = 8
VMEM_LIMIT_BYTES = 56 * 1024 * 1024

NORM_ROWS = 256
MM_TM = 2048
MM_TN = 1024
OUT_TM = 1024
OUT_TN = 1024
MLSTM_CHUNK = 256
ATTN_TILE = 256
ATTN_SUB = 4
NEAR_MAX_DIST = 512
NEAR_KEY_TILES = NEAR_MAX_DIST // ATTN_TILE + 1
FAR_WINDOW, FAR_DILATION = DILATION_PATTERNS[-1]
FAR_TILE = FAR_WINDOW // FAR_DILATION
assert all(w <= NEAR_MAX_DIST for w, _ in DILATION_PATTERNS[:-1])
assert NEAR_MAX_DIST % FAR_DILATION == 0 and FAR_TILE == LANES
CONV_HALO = SUBLANES
MASKED = -1e30
LOG2E = math.log2(math.e)

F32 = jnp.float32
BF16 = jnp.bfloat16
NT_DIMS = (((1,), (1,)), ((), ()))
RESIDENT = pl.Buffered(1)


def _params(*semantics):
    return pltpu.CompilerParams(dimension_semantics=semantics,
                                vmem_limit_bytes=VMEM_LIMIT_BYTES)


def _sigmoid(x):
    return 1.0 / (1.0 + jnp.exp2(x * (-LOG2E)))


def _silu(x):
    return x * _sigmoid(x)


def _log_sigmoid(x):
    return jnp.minimum(x, 0.0) - jnp.log1p(jnp.exp(-jnp.abs(x)))


def _rmsnorm_kernel(x_ref, g_ref, o_ref):
    x = x_ref[...]
    ms = jnp.mean(x * x, axis=-1, keepdims=True)
    o_ref[...] = (x * lax.rsqrt(ms + NORM_EPS) * g_ref[...]).astype(o_ref.dtype)


def _rmsnorm(x, g, out_dtype):
    s, d = x.shape
    return pl.pallas_call(
        _rmsnorm_kernel,
        out_shape=jax.ShapeDtypeStruct((s, d), out_dtype),
        grid=(s // NORM_ROWS,),
        in_specs=[pl.BlockSpec((NORM_ROWS, d), lambda i: (i, 0)),
                  pl.BlockSpec((1, d), lambda i: (0, 0))],
        out_specs=pl.BlockSpec((NORM_ROWS, d), lambda i: (i, 0)),
        compiler_params=_params("parallel"),
        name="rmsnorm",
    )(x, g.reshape(1, d))


def _matmul_nt_kernel(a_ref, w_ref, o_ref):
    half = a_ref.shape[0] // 2
    for r in range(2):
        rows = slice(r * half, (r + 1) * half)
        o_ref[rows, :] = lax.dot_general(a_ref[rows, :], w_ref[...], NT_DIMS,
                                         preferred_element_type=F32).astype(o_ref.dtype)


def _in_proj(h, wt, layer, n_cols):
    s, k = h.shape
    return pl.pallas_call(
        _matmul_nt_kernel,
        out_shape=jax.ShapeDtypeStruct((s, n_cols), BF16),
        grid=(s // MM_TM, n_cols // MM_TN),
        in_specs=[pl.BlockSpec((MM_TM, k), lambda i, j: (i, 0), pipeline_mode=RESIDENT),
                  pl.BlockSpec((None, MM_TN, k), lambda i, j: (layer, j, 0))],
        out_specs=pl.BlockSpec((MM_TM, MM_TN), lambda i, j: (i, j)),
        compiler_params=_params("parallel", "parallel"),
        name="in_proj",
    )(h, wt)


def _gates_kernel(h_ref, w_ref, bc_ref, br_ref, gc_ref, gr_ref):
    L = h_ref.shape[0]
    h = h_ref[...]
    w = w_ref[...]
    gc = lax.dot_general(h, w, NT_DIMS, preferred_element_type=F32) + bc_ref[...]
    gr = lax.dot_general(w[:2 * SUBLANES], h, NT_DIMS,
                         preferred_element_type=F32)[:NUM_GATES] + br_ref[...]
    row = lax.broadcasted_iota(jnp.int32, (L, L), 0)
    col = lax.broadcasted_iota(jnp.int32, (L, L), 1)
    lower = (col <= row).astype(F32)
    upper = (row <= col).astype(F32)

    lane = lax.broadcasted_iota(jnp.int32, gc.shape, 1)
    is_f = (lane >= MLSTM_HEADS) & (lane < NUM_GATES)
    lf_c = jnp.where(is_f, _log_sigmoid(gc), 0.0)
    b_c = jnp.dot(lower, lf_c, preferred_element_type=F32,
                  precision=lax.Precision.HIGHEST)
    gc_ref[...] = jnp.where(is_f, b_c, gc)

    sub = lax.broadcasted_iota(jnp.int32, gr.shape, 0)
    is_f_r = sub >= MLSTM_HEADS
    lf_r = jnp.where(is_f_r, _log_sigmoid(gr), 0.0)
    b_r = jnp.dot(lf_r, upper, preferred_element_type=F32,
                  precision=lax.Precision.HIGHEST)
    gr_ref[...] = jnp.where(is_f_r, b_r, gr)


def _gates(h, wt_gate, b_gate):
    s, k = h.shape
    L = MLSTM_CHUNK
    w = jnp.zeros((LANES, k), BF16).at[:NUM_GATES].set(wt_gate.astype(BF16))
    bc = jnp.zeros((1, LANES), F32).at[0, :NUM_GATES].set(b_gate)
    br = jnp.broadcast_to(b_gate.reshape(NUM_GATES, 1), (NUM_GATES, L))
    return pl.pallas_call(
        _gates_kernel,
        out_shape=(jax.ShapeDtypeStruct((s, LANES), F32),
                   jax.ShapeDtypeStruct((NUM_GATES, s), F32)),
        grid=(s // L,),
        in_specs=[pl.BlockSpec((L, k), lambda i: (i, 0)),
                  pl.BlockSpec((LANES, k), lambda i: (0, 0)),
                  pl.BlockSpec((1, LANES), lambda i: (0, 0)),
                  pl.BlockSpec((NUM_GATES, L), lambda i: (0, 0))],
        out_specs=(pl.BlockSpec((L, LANES), lambda i: (i, 0)),
                   pl.BlockSpec((NUM_GATES, L), lambda i: (0, i))),
        compiler_params=_params("parallel"),
        name="mlstm_gates",
    )(h, w, bc, br)


def _rel_buckets_np(dist):
    n = dist.astype(np.float32)
    large = REL_MAX_EXACT + np.floor(
        np.log(np.maximum(n, 1.0) / REL_MAX_EXACT)
        / np.log(REL_MAX_DISTANCE / REL_MAX_EXACT) * (REL_BUCKETS - REL_MAX_EXACT))
    large = np.minimum(large, REL_BUCKETS - 1)
    return np.where(n < REL_MAX_EXACT, n, large).astype(np.int32)


def _attn_bias_rows(rel_bias, tile, key_tiles, unit, lo, hi):
    i = np.arange(2 * tile)
    rel = np.where(i <= tile, -i, 2 * tile - i)
    dist = (np.asarray(key_tiles)[:, None] * tile + rel[None, :]) * unit
    mult = np.zeros(dist.shape, np.int32)
    for w, d in DILATION_PATTERNS:
        mult += ((dist >= lo) & (dist <= min(w, hi)) & (dist % d == 0)).astype(np.int32)
    bucket = _rel_buckets_np(np.maximum(dist, 0))
    logmult = np.log(np.maximum(mult, 1)).astype(np.float32)
    rows = (rel_bias.astype(F32).T[:, bucket] + logmult[None]) * LOG2E
    return jnp.where((mult > 0)[None], rows, MASKED)


def _bias_tiles_kernel(rows_ref, o_ref):
    nt, tile = o_ref.shape[1], o_ref.shape[2]
    for n in range(nt):
        wide = jnp.broadcast_to(rows_ref[0, n:n + 1, :], (tile, 2 * tile))
        o_ref[0, n] = pltpu.roll(wide, 0, 1, stride=1, stride_axis=0)[:, :tile]


def _attn_bias_tiles(rows, name):
    h, nt, two_tile = rows.shape
    tile = two_tile // 2
    return pl.pallas_call(
        _bias_tiles_kernel,
        out_shape=jax.ShapeDtypeStruct((h, nt, tile, tile), F32),
        grid=(h,),
        in_specs=[pl.BlockSpec((1, nt, 2 * tile), lambda i: (i, 0, 0))],
        out_specs=pl.BlockSpec((1, nt, tile, tile), lambda i: (i, 0, 0, 0)),
        compiler_params=_params("parallel"),
        name=name,
    )(rows)


def _near_bias_tiles(rel_bias):
    rows = _attn_bias_rows(rel_bias, ATTN_TILE, range(NEAR_KEY_TILES + 1), 1, 0, NEAR_MAX_DIST)
    return _attn_bias_tiles(rows, "attn_near_bias_tiles")


def _far_bias_tiles(rel_bias):
    rows = _attn_bias_rows(rel_bias, FAR_TILE, (1, 0), FAR_DILATION, NEAR_MAX_DIST + 1, FAR_WINDOW)
    return _attn_bias_tiles(rows, "attn_far_bias_tiles")


def _words_to_rows(w, half):
    bits = (w << 16) if half == 0 else (w & jnp.int32(-65536))
    return pltpu.bitcast(bits, F32)


def _rows_to_words(even, odd):
    lo = lax.shift_right_logical(pltpu.bitcast(even, jnp.int32), 16)
    return lo | (pltpu.bitcast(odd, jnp.int32) & jnp.int32(-65536))


def _attn_far_kernel(q_ref, k_ref, v_ref, bias_ref, o_ref, lse_ref, qw_ref, kw_ref, vw_ref):
    T = FAR_TILE
    D = ATTN_HEAD_DIM
    head = pl.program_id(0)
    n = q_ref.shape[0] // FAR_DILATION
    word_stride = FAR_DILATION // 2

    @pl.when(head == 0)
    def _():
        lse_ref[...] = jnp.zeros_like(lse_ref)

    qw_ref[...] = pltpu.bitcast(q_ref[...], jnp.int32)
    kw_ref[...] = pltpu.bitcast(k_ref[...], jnp.int32)
    vw_ref[...] = pltpu.bitcast(v_ref[...], jnp.int32)
    bias = jnp.concatenate([bias_ref[0, 0], bias_ref[0, 1]], axis=1)
    ones = jnp.ones((2 * T, D), BF16)
    lane = lax.broadcasted_iota(jnp.int32, (n, LANES), 1)

    def residue_pair(rp, carry):
        words = pl.ds(rp, n, stride=word_stride)
        wq, wk, wv = qw_ref[words, :], kw_ref[words, :], vw_ref[words, :]
        outs = []
        for half in range(2):
            q = (_words_to_rows(wq, half) * (D ** -0.5 * LOG2E)).astype(BF16)
            k = _words_to_rows(wk, half).astype(BF16)
            v = _words_to_rows(wv, half).astype(BF16)
            o_blocks, lse_blocks = [], []
            for b in range(n // T):
                k0 = max(b - 1, 0) * T
                nk = (b + 1) * T - k0
                s = lax.dot_general(q[b * T:(b + 1) * T], k[k0:k0 + nk], NT_DIMS,
                                    preferred_element_type=F32) + bias[:, 2 * T - nk:]
                m = jnp.max(s, axis=-1, keepdims=True)
                p = jnp.exp2(s - m).astype(BF16)
                acc = jnp.dot(p, jnp.concatenate([v[k0:k0 + nk], ones[:nk]], axis=1),
                              preferred_element_type=F32)
                o_blocks.append(acc[:, :D] / acc[:, D:])
                lse_blocks.append(m + jnp.log2(acc[:, D:D + 1]))
            lse = jnp.concatenate(lse_blocks, axis=0)
            rows = pl.ds(2 * rp + half, n, stride=FAR_DILATION)
            lse_ref[rows, :] = jnp.where(lane == head, lse, lse_ref[rows, :])
            outs.append(jnp.concatenate(o_blocks, axis=0).astype(BF16).astype(F32))
        o_ref[words, :] = _rows_to_words(outs[0], outs[1])
        return carry

    lax.fori_loop(0, word_stride, residue_pair, 0)


def _attention_far(u, bias_tiles):
    s = u.shape[0]
    H = ATTN_HEADS
    D = ATTN_HEAD_DIM
    T = FAR_TILE
    return pl.pallas_call(
        _attn_far_kernel,
        out_shape=(jax.ShapeDtypeStruct((s // 2, ATTN_WIDTH), jnp.int32),
                   jax.ShapeDtypeStruct((s, LANES), F32)),
        grid=(H,),
        in_specs=[pl.BlockSpec((s, D), lambda h: (0, h)),
                  pl.BlockSpec((s, D), lambda h: (0, H + h)),
                  pl.BlockSpec((s, D), lambda h: (0, 2 * H + h)),
                  pl.BlockSpec((1, 2, T, T), lambda h: (h, 0, 0, 0))],
        out_specs=(pl.BlockSpec((s // 2, D), lambda h: (0, h)),
                   pl.BlockSpec((s, LANES), lambda h: (0, 0))),
        scratch_shapes=[pltpu.VMEM((s // 2, D), jnp.int32)] * 3,
        compiler_params=_params("arbitrary"),
        name="dilated_attention_far",
    )(u, u, u, bias_tiles)


def _attn_near_kernel(q_ref, k_ref, v_ref, z_ref, bias_ref, ofar_ref, lse_ref, o_ref, s_ref):
    T = ATTN_TILE
    D = ATTN_HEAD_DIM
    head = pl.program_id(0)
    step = pl.program_id(1)
    ones = jnp.ones((T, D), BF16)
    lane = lax.broadcasted_iota(jnp.int32, (T, LANES), 1)
    for u in range(ATTN_SUB):
        rows = slice(u * T, (u + 1) * T)
        j = step * ATTN_SUB + u
        q = (q_ref[rows, :].astype(F32) * (D ** -0.5 * LOG2E)).astype(BF16)

        def key_start(kt):
            return pl.multiple_of(jnp.maximum(j - kt, 0) * T, T)

        m = jnp.sum(jnp.where(lane == head, lse_ref[rows, :], 0.0), axis=1, keepdims=True)
        lse_far = m
        for kt in range(NEAR_KEY_TILES):
            bias_idx = jnp.where(kt <= j, kt, NEAR_KEY_TILES)
            k = k_ref[pl.ds(key_start(kt), T), :]
            s = lax.dot_general(q, k, NT_DIMS, preferred_element_type=F32) + bias_ref[0, bias_idx]
            s_ref[u, kt] = s
            m = jnp.maximum(m, jnp.max(s, axis=-1, keepdims=True))

        acc = jnp.zeros((T, 2 * D), F32)
        for kt in range(NEAR_KEY_TILES):
            p = jnp.exp2(s_ref[u, kt] - m).astype(BF16)
            v = v_ref[pl.ds(key_start(kt), T), :]
            acc = acc + jnp.dot(p, jnp.concatenate([v, ones], axis=1),
                                preferred_element_type=F32)
        w_far = jnp.exp2(lse_far - m)
        o_far = pltpu.bitcast(ofar_ref[u * T // 2:(u + 1) * T // 2, :], BF16).astype(F32)
        o = (acc[:, :D] + w_far * o_far) / (acc[:, D:] + w_far)
        o_ref[rows, :] = (o * _silu(z_ref[rows, :].astype(F32))).astype(o_ref.dtype)


def _attention_near(u, bias_tiles, o_far, lse_far):
    s = u.shape[0]
    T = ATTN_TILE
    R = ATTN_SUB * T
    H = ATTN_HEADS
    D = ATTN_HEAD_DIM
    nt = NEAR_KEY_TILES + 1
    return pl.pallas_call(
        _attn_near_kernel,
        out_shape=jax.ShapeDtypeStruct((s, ATTN_WIDTH), BF16),
        grid=(H, s // R),
        in_specs=[pl.BlockSpec((R, D), lambda h, i: (i, h)),
                  pl.BlockSpec((s, D), lambda h, i: (0, H + h)),
                  pl.BlockSpec((s, D), lambda h, i: (0, 2 * H + h)),
                  pl.BlockSpec((R, D), lambda h, i: (i, 3 * H + h)),
                  pl.BlockSpec((1, nt, T, T), lambda h, i: (h, 0, 0, 0)),
                  pl.BlockSpec((R // 2, D), lambda h, i: (i, h)),
                  pl.BlockSpec((R, LANES), lambda h, i: (i, 0))],
        out_specs=pl.BlockSpec((R, D), lambda h, i: (i, h)),
        scratch_shapes=[pltpu.VMEM((ATTN_SUB, NEAR_KEY_TILES, T, T), F32)],
        compiler_params=_params("parallel", "parallel"),
        name="dilated_attention_near",
    )(u, u, u, u, bias_tiles, o_far, lse_far)


def _attention(u, near_bias, far_bias):
    o_far, lse_far = _attention_far(u, far_bias)
    return _attention_near(u, near_bias, o_far, lse_far)


def _conv_silu(ext_ref, x_ref, w_ref, cols):
    L = x_ref.shape[0]
    ext_ref[CONV_HALO:CONV_HALO + L, cols] = x_ref[:, cols].astype(F32)
    acc = None
    for j in range(CONV_WIDTH):
        back = CONV_WIDTH - 1 - j
        term = ext_ref[pl.ds(CONV_HALO - back, L), cols] * w_ref[j:j + 1, cols]
        acc = term if acc is None else acc + term
    ext_ref[0:CONV_HALO, cols] = ext_ref[L:L + CONV_HALO, cols]
    return _silu(acc)


def _mlstm_head(hd, q_ref, k_ref, v_ref, og_ref, zg_ref, gc, gr, wq_ref, wk_ref,
                ng_ref, o_ref, c_ref, n_ref, m_ref, extq_ref, extk_ref, causal):
    L = q_ref.shape[0]
    dk, dv = MLSTM_QK_DIM, MLSTM_V_DIM
    kcols = slice(hd * dk, (hd + 1) * dk)
    vcols = slice(hd * dv, (hd + 1) * dv)

    q = _conv_silu(extq_ref, q_ref, wq_ref, kcols)
    k = _conv_silu(extk_ref, k_ref, wk_ref, kcols) * (dk ** -0.5)
    qb = q.astype(BF16)
    kb = k.astype(BF16)
    vb = v_ref[:, vcols]

    li_c = gc[:, hd:hd + 1]
    b_c = gc[:, MLSTM_HEADS + hd:MLSTM_HEADS + hd + 1]
    li_r = gr[hd:hd + 1, :]
    b_r = gr[MLSTM_HEADS + hd:MLSTM_HEADS + hd + 1, :]

    m_prev = m_ref[hd, :, 0:1]
    dmat = jnp.where(causal, b_c - b_r + li_r, MASKED)
    m_inter = b_c + m_prev
    m_t = jnp.maximum(m_inter, jnp.max(dmat, axis=1, keepdims=True))
    w_inter = jnp.exp(m_inter - m_t)
    sqk = lax.dot_general(qb, kb, NT_DIMS, preferred_element_type=F32) * jnp.exp(dmat - m_t)
    num = (w_inter * jnp.dot(qb, c_ref[hd].astype(BF16), preferred_element_type=F32)
           + jnp.dot(sqk.astype(BF16), vb, preferred_element_type=F32))
    nq = (w_inter * jnp.sum(q * n_ref[hd], axis=1, keepdims=True)
          + jnp.sum(sqk, axis=1, keepdims=True))
    rden = 1.0 / jnp.maximum(jnp.abs(nq), jnp.exp(-m_t))
    ms = jnp.mean(num * num, axis=1, keepdims=True)
    row_scale = rden * lax.rsqrt(ms * rden * rden + NORM_EPS)
    gate = _sigmoid(og_ref[:, vcols].astype(F32)) * _silu(zg_ref[:, vcols].astype(F32))
    o_ref[:, vcols] = (num * row_scale * ng_ref[:, vcols] * gate).astype(o_ref.dtype)

    b_last = b_c[L - 1:L, :]
    g = b_last - b_c + li_c
    m_new = jnp.maximum(b_last + m_prev, jnp.max(g, axis=0, keepdims=True))
    decay = jnp.exp(b_last + m_prev - m_new)
    kw = k * jnp.exp(g - m_new)
    c_ref[hd] = decay * c_ref[hd] + lax.dot_general(
        kw.astype(BF16), vb, (((0,), (0,)), ((), ())), preferred_element_type=F32)
    n_ref[hd] = decay * n_ref[hd] + jnp.sum(kw, axis=0, keepdims=True)
    m_ref[hd] = jnp.broadcast_to(m_new, m_ref.shape[1:])


def _mlstm_kernel(q_ref, k_ref, v_ref, og_ref, zg_ref, gc_ref, gr_ref,
                  wq_ref, wk_ref, ng_ref, o_ref,
                  c_ref, n_ref, m_ref, extq_ref, extk_ref):
    L = q_ref.shape[0]

    @pl.when(pl.program_id(0) == 0)
    def _():
        c_ref[...] = jnp.zeros_like(c_ref)
        n_ref[...] = jnp.zeros_like(n_ref)
        m_ref[...] = jnp.zeros_like(m_ref)
        extq_ref[0:CONV_HALO, :] = jnp.zeros((CONV_HALO, extq_ref.shape[1]), F32)
        extk_ref[0:CONV_HALO, :] = jnp.zeros((CONV_HALO, extk_ref.shape[1]), F32)

    gc = gc_ref[...]
    gr = gr_ref[...]
    row = lax.broadcasted_iota(jnp.int32, (L, L), 0)
    col = lax.broadcasted_iota(jnp.int32, (L, L), 1)
    causal = col <= row
    for hd in range(MLSTM_HEADS):
        _mlstm_head(hd, q_ref, k_ref, v_ref, og_ref, zg_ref, gc, gr, wq_ref, wk_ref,
                    ng_ref, o_ref, c_ref, n_ref, m_ref, extq_ref, extk_ref, causal)


def _mlstm(u, gc, gr, conv_w, norm_g):
    s = u.shape[0]
    L = MLSTM_CHUNK
    H = MLSTM_HEADS
    dk, dv = MLSTM_QK_DIM, MLSTM_V_DIM
    qkw, vw = MLSTM_QK_WIDTH, MLSTM_WIDTH
    q0 = MLSTM_COL0 // qkw
    k0 = q0 + 1
    v0 = (MLSTM_COL0 + 2 * qkw) // vw
    og0 = v0 + 1
    zg0 = og0 + 1
    return pl.pallas_call(
        _mlstm_kernel,
        out_shape=jax.ShapeDtypeStruct((s, vw), BF16),
        grid=(s // L,),
        in_specs=[pl.BlockSpec((L, qkw), lambda c: (c, q0)),
                  pl.BlockSpec((L, qkw), lambda c: (c, k0)),
                  pl.BlockSpec((L, vw), lambda c: (c, v0)),
                  pl.BlockSpec((L, vw), lambda c: (c, og0)),
                  pl.BlockSpec((L, vw), lambda c: (c, zg0)),
                  pl.BlockSpec((L, LANES), lambda c: (c, 0)),
                  pl.BlockSpec((NUM_GATES, L), lambda c: (0, c)),
                  pl.BlockSpec((CONV_WIDTH, qkw), lambda c: (0, 0)),
                  pl.BlockSpec((CONV_WIDTH, qkw), lambda c: (0, 1)),
                  pl.BlockSpec((1, vw), lambda c: (0, 0))],
        out_specs=pl.BlockSpec((L, vw), lambda c: (c, 0)),
        scratch_shapes=[pltpu.VMEM((H, dk, dv), F32),
                        pltpu.VMEM((H, 1, dk), F32),
                        pltpu.VMEM((H, 1, LANES), F32),
                        pltpu.VMEM((CONV_HALO + L, qkw), F32),
                        pltpu.VMEM((CONV_HALO + L, qkw), F32)],
        compiler_params=_params("arbitrary"),
        name="mlstm",
    )(u, u, u, u, u, gc, gr, conv_w, conv_w, norm_g.reshape(1, vw))


def _out_proj_kernel(x_ref, ya_ref, ym_ref, wa_ref, wm_ref, o_ref):
    o_ref[...] = (x_ref[...]
                  + jnp.dot(ya_ref[...], wa_ref[...], preferred_element_type=F32)
                  + jnp.dot(ym_ref[...], wm_ref[...], preferred_element_type=F32))


def _out_proj(x, ya, ym, w, layer):
    s, d = x.shape
    ka, km = ya.shape[1], ym.shape[1]
    assert ka == km
    return pl.pallas_call(
        _out_proj_kernel,
        out_shape=jax.ShapeDtypeStruct((s, d), F32),
        grid=(s // OUT_TM, d // OUT_TN),
        in_specs=[pl.BlockSpec((OUT_TM, OUT_TN), lambda i, j: (i, j)),
                  pl.BlockSpec((OUT_TM, ka), lambda i, j: (i, 0), pipeline_mode=RESIDENT),
                  pl.BlockSpec((OUT_TM, km), lambda i, j: (i, 0), pipeline_mode=RESIDENT),
                  pl.BlockSpec((None, ka, OUT_TN), lambda i, j: (layer, 0, j)),
                  pl.BlockSpec((None, km, OUT_TN), lambda i, j: (layer, 1, j))],
        out_specs=pl.BlockSpec((OUT_TM, OUT_TN), lambda i, j: (i, j)),
        compiler_params=_params("parallel", "parallel"),
        name="out_proj",
    )(x, ya, ym, w, w)


def kernel(x, norm_g, w_in, b_gate, conv_w, mlstm_norm_g, w_out, rel_bias, final_g):
    b, s, d = x.shape
    assert b == 1
    xs = x.reshape(s, d)
    near_bias = _near_bias_tiles(rel_bias)
    far_bias = _far_bias_tiles(rel_bias)
    wt_in = jnp.swapaxes(w_in, 1, 2).astype(BF16)
    w_out_b = w_out.astype(BF16)
    for l in range(DEPTH):
        h = _rmsnorm(xs, norm_g[l], BF16)
        u = _in_proj(h, wt_in, l, GATE_COL0)
        gc, gr = _gates(h, wt_in[l, GATE_COL0:], b_gate[l])
        ya = _attention(u, near_bias, far_bias)
        ym = _mlstm(u, gc, gr, conv_w[l], mlstm_norm_g[l])
        xs = _out_proj(xs, ya, ym, w_out_b, l)
    return _rmsnorm(xs, final_g, F32).reshape(b, s, d)
```

```python
import math

import jax
import jax.numpy as jnp
import numpy as np
from jax import lax
from jax.experimental import pallas as pl
from jax.experimental.pallas import tpu as pltpu

D_MODEL = 4096
DEPTH = 2
ATTN_WIDTH = 2048
ATTN_HEAD_DIM = 128
ATTN_HEADS = ATTN_WIDTH // ATTN_HEAD_DIM
DILATION_PATTERNS = ((128, 1), (512, 4), (2048, 16))
REL_BUCKETS = 32
REL_MAX_EXACT = REL_BUCKETS // 2
REL_MAX_DISTANCE = 2048
MLSTM_WIDTH = 2048
MLSTM_HEADS = 4
MLSTM_V_DIM = MLSTM_WIDTH // MLSTM_HEADS
MLSTM_QK_DIM = MLSTM_V_DIM // 2
MLSTM_QK_WIDTH = MLSTM_HEADS * MLSTM_QK_DIM
CONV_WIDTH = 4
NORM_EPS = 1e-6
ATTN_COLS = 4 * ATTN_WIDTH
MLSTM_COL0 = ATTN_COLS
MLSTM_COLS = 2 * MLSTM_QK_WIDTH + 3 * MLSTM_WIDTH
GATE_COL0 = ATTN_COLS + MLSTM_COLS
NUM_GATES = 2 * MLSTM_HEADS

LANES = 128
SUBLANES = 8
VMEM_LIMIT_BYTES = 56 * 1024 * 1024

NORM_ROWS = 256
MM_TM = 2048
MM_TN = 512
OUT_TM = 1024
OUT_TN = 512
MLSTM_CHUNK = 256
ATTN_TILE = 256
ATTN_SUB = 4
NEAR_MAX_DIST = 512
NEAR_KEY_TILES = NEAR_MAX_DIST // ATTN_TILE + 1
FAR_WINDOW, FAR_DILATION = DILATION_PATTERNS[-1]
FAR_TILE = FAR_WINDOW // FAR_DILATION
assert all(w <= NEAR_MAX_DIST for w, _ in DILATION_PATTERNS[:-1])
assert NEAR_MAX_DIST % FAR_DILATION == 0 and FAR_TILE == LANES
CONV_HALO = SUBLANES
MASKED = -1e30
LOG2E = math.log2(math.e)

F32 = jnp.float32
BF16 = jnp.bfloat16
NT_DIMS = (((1,), (1,)), ((), ()))
RESIDENT = pl.Buffered(1)


def _params(*semantics):
    return pltpu.CompilerParams(dimension_semantics=semantics,
                                vmem_limit_bytes=VMEM_LIMIT_BYTES)


def _sigmoid(x):
    return 1.0 / (1.0 + jnp.exp2(x * (-LOG2E)))


def _silu(x):
    return x * _sigmoid(x)


def _log_sigmoid(x):
    return jnp.minimum(x, 0.0) - jnp.log1p(jnp.exp(-jnp.abs(x)))


def _rmsnorm_kernel(x_ref, g_ref, o_ref):
    x = x_ref[...]
    ms = jnp.mean(x * x, axis=-1, keepdims=True)
    o_ref[...] = (x * lax.rsqrt(ms + NORM_EPS) * g_ref[...]).astype(o_ref.dtype)


def _rmsnorm(x, g, out_dtype):
    s, d = x.shape
    return pl.pallas_call(
        _rmsnorm_kernel,
        out_shape=jax.ShapeDtypeStruct((s, d), out_dtype),
        grid=(s // NORM_ROWS,),
        in_specs=[pl.BlockSpec((NORM_ROWS, d), lambda i: (i, 0)),
                  pl.BlockSpec((1, d), lambda i: (0, 0))],
        out_specs=pl.BlockSpec((NORM_ROWS, d), lambda i: (i, 0)),
        compiler_params=_params("parallel"),
        name="rmsnorm",
    )(x, g.reshape(1, d))


def _matmul_nt_kernel(a_ref, w_ref, o_ref):
    w = w_ref[...].astype(BF16)
    half = a_ref.shape[0] // 2
    for r in range(2):
        rows = slice(r * half, (r + 1) * half)
        o_ref[rows, :] = lax.dot_general(a_ref[rows, :], w, NT_DIMS,
                                         preferred_element_type=F32).astype(o_ref.dtype)


def _in_proj(h, wt, layer, n_cols):
    s, k = h.shape
    return pl.pallas_call(
        _matmul_nt_kernel,
        out_shape=jax.ShapeDtypeStruct((s, n_cols), BF16),
        grid=(s // MM_TM, n_cols // MM_TN),
        in_specs=[pl.BlockSpec((MM_TM, k), lambda i, j: (i, 0), pipeline_mode=RESIDENT),
                  pl.BlockSpec((None, MM_TN, k), lambda i, j: (layer, j, 0))],
        out_specs=pl.BlockSpec((MM_TM, MM_TN), lambda i, j: (i, j)),
        compiler_params=_params("parallel", "parallel"),
        name="in_proj",
    )(h, wt)


def _norm_gates_kernel(x_ref, g_ref, w_ref, bc_ref, br_ref, h_ref, gc_ref, gr_ref):
    L = x_ref.shape[0]
    x = x_ref[...]
    ms = jnp.mean(x * x, axis=-1, keepdims=True)
    h = (x * lax.rsqrt(ms + NORM_EPS) * g_ref[...]).astype(BF16)
    h_ref[...] = h
    w = w_ref[...]
    gc = lax.dot_general(h, w, NT_DIMS, preferred_element_type=F32) + bc_ref[...]
    gr = lax.dot_general(w[:2 * SUBLANES], h, NT_DIMS,
                         preferred_element_type=F32)[:NUM_GATES] + br_ref[...]
    row = lax.broadcasted_iota(jnp.int32, (L, L), 0)
    col = lax.broadcasted_iota(jnp.int32, (L, L), 1)
    lower = (col <= row).astype(F32)
    upper = (row <= col).astype(F32)

    lane = lax.broadcasted_iota(jnp.int32, gc.shape, 1)
    is_f = (lane >= MLSTM_HEADS) & (lane < NUM_GATES)
    lf_c = jnp.where(is_f, _log_sigmoid(gc), 0.0)
    b_c = jnp.dot(lower, lf_c, preferred_element_type=F32,
                  precision=lax.Precision.HIGHEST)
    gc_ref[...] = jnp.where(is_f, b_c, gc)

    sub = lax.broadcasted_iota(jnp.int32, gr.shape, 0)
    is_f_r = sub >= MLSTM_HEADS
    lf_r = jnp.where(is_f_r, _log_sigmoid(gr), 0.0)
    b_r = jnp.dot(lf_r, upper, preferred_element_type=F32,
                  precision=lax.Precision.HIGHEST)
    gr_ref[...] = jnp.where(is_f_r, b_r, gr)


def _norm_gates(x, g, wt_gate, b_gate):
    s, k = x.shape
    L = MLSTM_CHUNK
    w = jnp.zeros((LANES, k), BF16).at[:NUM_GATES].set(wt_gate.astype(BF16))
    bc = jnp.zeros((1, LANES), F32).at[0, :NUM_GATES].set(b_gate)
    br = jnp.broadcast_to(b_gate.reshape(NUM_GATES, 1), (NUM_GATES, L))
    return pl.pallas_call(
        _norm_gates_kernel,
        out_shape=(jax.ShapeDtypeStruct((s, k), BF16),
                   jax.ShapeDtypeStruct((s, LANES), F32),
                   jax.ShapeDtypeStruct((NUM_GATES, s), F32)),
        grid=(s // L,),
        in_specs=[pl.BlockSpec((L, k), lambda i: (i, 0)),
                  pl.BlockSpec((1, k), lambda i: (0, 0)),
                  pl.BlockSpec((LANES, k), lambda i: (0, 0)),
                  pl.BlockSpec((1, LANES), lambda i: (0, 0)),
                  pl.BlockSpec((NUM_GATES, L), lambda i: (0, 0))],
        out_specs=(pl.BlockSpec((L, k), lambda i: (i, 0)),
                   pl.BlockSpec((L, LANES), lambda i: (i, 0)),
                   pl.BlockSpec((NUM_GATES, L), lambda i: (0, i))),
        compiler_params=_params("parallel"),
        name="norm_gates",
    )(x, g.reshape(1, k), w, bc, br)


def _rel_buckets_np(dist):
    n = dist.astype(np.float32)
    large = REL_MAX_EXACT + np.floor(
        np.log(np.maximum(n, 1.0) / REL_MAX_EXACT)
        / np.log(REL_MAX_DISTANCE / REL_MAX_EXACT) * (REL_BUCKETS - REL_MAX_EXACT))
    large = np.minimum(large, REL_BUCKETS - 1)
    return np.where(n < REL_MAX_EXACT, n, large).astype(np.int32)


def _attn_bias_rows(rel_bias, tile, key_tiles, unit, lo, hi):
    i = np.arange(2 * tile)
    rel = np.where(i <= tile, -i, 2 * tile - i)
    dist = (np.asarray(key_tiles)[:, None] * tile + rel[None, :]) * unit
    mult = np.zeros(dist.shape, np.int32)
    for w, d in DILATION_PATTERNS:
        mult += ((dist >= lo) & (dist <= min(w, hi)) & (dist % d == 0)).astype(np.int32)
    bucket = _rel_buckets_np(np.maximum(dist, 0))
    logmult = np.log(np.maximum(mult, 1)).astype(np.float32)
    rows = (rel_bias.astype(F32).T[:, bucket] + logmult[None]) * LOG2E
    return jnp.where((mult > 0)[None], rows, MASKED)


def _bias_tiles_kernel(rows_ref, o_ref):
    nt, tile = o_ref.shape[1], o_ref.shape[2]
    for n in range(nt):
        wide = jnp.broadcast_to(rows_ref[0, n:n + 1, :], (tile, 2 * tile))
        o_ref[0, n] = pltpu.roll(wide, 0, 1, stride=1, stride_axis=0)[:, :tile]


def _attn_bias_tiles(rows, name):
    h, nt, two_tile = rows.shape
    tile = two_tile // 2
    return pl.pallas_call(
        _bias_tiles_kernel,
        out_shape=jax.ShapeDtypeStruct((h, nt, tile, tile), F32),
        grid=(h,),
        in_specs=[pl.BlockSpec((1, nt, 2 * tile), lambda i: (i, 0, 0))],
        out_specs=pl.BlockSpec((1, nt, tile, tile), lambda i: (i, 0, 0, 0)),
        compiler_params=_params("parallel"),
        name=name,
    )(rows)


def _near_bias_tiles(rel_bias):
    rows = _attn_bias_rows(rel_bias, ATTN_TILE, range(NEAR_KEY_TILES + 1), 1, 0, NEAR_MAX_DIST)
    return _attn_bias_tiles(rows, "attn_near_bias_tiles")


def _far_bias_tiles(rel_bias):
    rows = _attn_bias_rows(rel_bias, FAR_TILE, (1, 0), FAR_DILATION, NEAR_MAX_DIST + 1, FAR_WINDOW)
    return _attn_bias_tiles(rows, "attn_far_bias_tiles")


def _words_to_rows(w, half):
    bits = (w << 16) if half == 0 else (w & jnp.int32(-65536))
    return pltpu.bitcast(bits, F32)


def _rows_to_words(even, odd):
    lo = lax.shift_right_logical(pltpu.bitcast(even, jnp.int32), 16)
    return lo | (pltpu.bitcast(odd, jnp.int32) & jnp.int32(-65536))


def _attn_far_kernel(q_ref, k_ref, v_ref, bias_ref, o_ref, lse_ref, qw_ref, kw_ref, vw_ref):
    T = FAR_TILE
    D = ATTN_HEAD_DIM
    head = pl.program_id(0)
    n = q_ref.shape[0] // FAR_DILATION
    word_stride = FAR_DILATION // 2

    @pl.when(head == 0)
    def _():
        lse_ref[...] = jnp.zeros_like(lse_ref)

    qw_ref[...] = pltpu.bitcast(q_ref[...], jnp.int32)
    kw_ref[...] = pltpu.bitcast(k_ref[...], jnp.int32)
    vw_ref[...] = pltpu.bitcast(v_ref[...], jnp.int32)
    bias = jnp.concatenate([bias_ref[0, 0], bias_ref[0, 1]], axis=1)
    ones = jnp.ones((2 * T, D), BF16)
    lane = lax.broadcasted_iota(jnp.int32, (n, LANES), 1)

    def residue_pair(rp, carry):
        words = pl.ds(rp, n, stride=word_stride)
        wq, wk, wv = qw_ref[words, :], kw_ref[words, :], vw_ref[words, :]
        outs = []
        for half in range(2):
            q = (_words_to_rows(wq, half) * (D ** -0.5 * LOG2E)).astype(BF16)
            k = _words_to_rows(wk, half).astype(BF16)
            v = _words_to_rows(wv, half).astype(BF16)
            o_blocks, lse_blocks = [], []
            for b in range(n // T):
                k0 = max(b - 1, 0) * T
                nk = (b + 1) * T - k0
                s = lax.dot_general(q[b * T:(b + 1) * T], k[k0:k0 + nk], NT_DIMS,
                                    preferred_element_type=F32) + bias[:, 2 * T - nk:]
                m = jnp.max(s, axis=-1, keepdims=True)
                p = jnp.exp2(s - m).astype(BF16)
                acc = jnp.dot(p, jnp.concatenate([v[k0:k0 + nk], ones[:nk]], axis=1),
                              preferred_element_type=F32)
                o_blocks.append(acc[:, :D] / acc[:, D:])
                lse_blocks.append(m + jnp.log2(acc[:, D:D + 1]))
            lse = jnp.concatenate(lse_blocks, axis=0)
            rows = pl.ds(2 * rp + half, n, stride=FAR_DILATION)
            lse_ref[rows, :] = jnp.where(lane == head, lse, lse_ref[rows, :])
            outs.append(jnp.concatenate(o_blocks, axis=0).astype(BF16).astype(F32))
        o_ref[words, :] = _rows_to_words(outs[0], outs[1])
        return carry

    lax.fori_loop(0, word_stride, residue_pair, 0)


def _attention_far(u, bias_tiles):
    s = u.shape[0]
    H = ATTN_HEADS
    D = ATTN_HEAD_DIM
    T = FAR_TILE
    return pl.pallas_call(
        _attn_far_kernel,
        out_shape=(jax.ShapeDtypeStruct((s // 2, ATTN_WIDTH), jnp.int32),
                   jax.ShapeDtypeStruct((s, LANES), F32)),
        grid=(H,),
        in_specs=[pl.BlockSpec((s, D), lambda h: (0, h)),
                  pl.BlockSpec((s, D), lambda h: (0, H + h)),
                  pl.BlockSpec((s, D), lambda h: (0, 2 * H + h)),
                  pl.BlockSpec((1, 2, T, T), lambda h: (h, 0, 0, 0))],
        out_specs=(pl.BlockSpec((s // 2, D), lambda h: (0, h)),
                   pl.BlockSpec((s, LANES), lambda h: (0, 0))),
        scratch_shapes=[pltpu.VMEM((s // 2, D), jnp.int32)] * 3,
        compiler_params=_params("arbitrary"),
        name="dilated_attention_far",
    )(u, u, u, bias_tiles)


def _attn_near_kernel(q_ref, k_ref, v_ref, z_ref, bias_ref, ofar_ref, lse_ref, o_ref, s_ref):
    T = ATTN_TILE
    D = ATTN_HEAD_DIM
    head = pl.program_id(0)
    step = pl.program_id(1)
    ones = jnp.ones((T, D), BF16)
    lane = lax.broadcasted_iota(jnp.int32, (T, LANES), 1)
    for u in range(ATTN_SUB):
        rows = slice(u * T, (u + 1) * T)
        j = step * ATTN_SUB + u
        q = (q_ref[rows, :].astype(F32) * (D ** -0.5 * LOG2E)).astype(BF16)

        def key_start(kt):
            return pl.multiple_of(jnp.maximum(j - kt, 0) * T, T)

        m = jnp.sum(jnp.where(lane == head, lse_ref[rows, :], 0.0), axis=1, keepdims=True)
        lse_far = m
        for kt in range(NEAR_KEY_TILES):
            bias_idx = jnp.where(kt <= j, kt, NEAR_KEY_TILES)
            k = k_ref[pl.ds(key_start(kt), T), :]
            s = lax.dot_general(q, k, NT_DIMS, preferred_element_type=F32) + bias_ref[0, bias_idx]
            s_ref[u, kt] = s
            m = jnp.maximum(m, jnp.max(s, axis=-1, keepdims=True))

        acc = jnp.zeros((T, 2 * D), F32)
        for kt in range(NEAR_KEY_TILES):
            p = jnp.exp2(s_ref[u, kt] - m).astype(BF16)
            v = v_ref[pl.ds(key_start(kt), T), :]
            acc = acc + jnp.dot(p, jnp.concatenate([v, ones], axis=1),
                                preferred_element_type=F32)
        w_far = jnp.exp2(lse_far - m)
        o_far = pltpu.bitcast(ofar_ref[u * T // 2:(u + 1) * T // 2, :], BF16).astype(F32)
        o = (acc[:, :D] + w_far * o_far) / (acc[:, D:] + w_far)
        o_ref[rows, :] = (o * _silu(z_ref[rows, :].astype(F32))).astype(o_ref.dtype)


def _attention_near(u, bias_tiles, o_far, lse_far):
    s = u.shape[0]
    T = ATTN_TILE
    R = ATTN_SUB * T
    H = ATTN_HEADS
    D = ATTN_HEAD_DIM
    nt = NEAR_KEY_TILES + 1
    return pl.pallas_call(
        _attn_near_kernel,
        out_shape=jax.ShapeDtypeStruct((s, ATTN_WIDTH), BF16),
        grid=(H, s // R),
        in_specs=[pl.BlockSpec((R, D), lambda h, i: (i, h)),
                  pl.BlockSpec((s, D), lambda h, i: (0, H + h)),
                  pl.BlockSpec((s, D), lambda h, i: (0, 2 * H + h)),
                  pl.BlockSpec((R, D), lambda h, i: (i, 3 * H + h)),
                  pl.BlockSpec((1, nt, T, T), lambda h, i: (h, 0, 0, 0)),
                  pl.BlockSpec((R // 2, D), lambda h, i: (i, h)),
                  pl.BlockSpec((R, LANES), lambda h, i: (i, 0))],
        out_specs=pl.BlockSpec((R, D), lambda h, i: (i, h)),
        scratch_shapes=[pltpu.VMEM((ATTN_SUB, NEAR_KEY_TILES, T, T), F32)],
        compiler_params=_params("parallel", "parallel"),
        name="dilated_attention_near",
    )(u, u, u, u, bias_tiles, o_far, lse_far)


def _attention(u, near_bias, far_bias):
    o_far, lse_far = _attention_far(u, far_bias)
    return _attention_near(u, near_bias, o_far, lse_far)


def _conv_silu(ext_ref, x_ref, w_ref, cols):
    L = x_ref.shape[0]
    ext_ref[CONV_HALO:CONV_HALO + L, cols] = x_ref[:, cols].astype(F32)
    acc = None
    for j in range(CONV_WIDTH):
        back = CONV_WIDTH - 1 - j
        term = ext_ref[pl.ds(CONV_HALO - back, L), cols] * w_ref[j:j + 1, cols]
        acc = term if acc is None else acc + term
    ext_ref[0:CONV_HALO, cols] = ext_ref[L:L + CONV_HALO, cols]
    return _silu(acc)


def _mlstm_head(hd, q_ref, k_ref, v_ref, og_ref, zg_ref, gc, gr, wq_ref, wk_ref,
                ng_ref, o_ref, c_ref, n_ref, m_ref, extq_ref, extk_ref, causal):
    L = q_ref.shape[0]
    dk, dv = MLSTM_QK_DIM, MLSTM_V_DIM
    kcols = slice(hd * dk, (hd + 1) * dk)
    vcols = slice(hd * dv, (hd + 1) * dv)

    q = _conv_silu(extq_ref, q_ref, wq_ref, kcols)
    k = _conv_silu(extk_ref, k_ref, wk_ref, kcols) * (dk ** -0.5)
    qb = q.astype(BF16)
    kb = k.astype(BF16)
    vb = v_ref[:, vcols]

    li_c = gc[:, hd:hd + 1]
    b_c = gc[:, MLSTM_HEADS + hd:MLSTM_HEADS + hd + 1]
    li_r = gr[hd:hd + 1, :]
    b_r = gr[MLSTM_HEADS + hd:MLSTM_HEADS + hd + 1, :]

    m_prev = m_ref[hd, :, 0:1]
    dmat = jnp.where(causal, b_c - b_r + li_r, MASKED)
    m_inter = b_c + m_prev
    m_t = jnp.maximum(m_inter, jnp.max(dmat, axis=1, keepdims=True))
    w_inter = jnp.exp(m_inter - m_t)
    sqk = lax.dot_general(qb, kb, NT_DIMS, preferred_element_type=F32) * jnp.exp(dmat - m_t)
    num = (w_inter * jnp.dot(qb, c_ref[hd].astype(BF16), preferred_element_type=F32)
           + jnp.dot(sqk.astype(BF16), vb, preferred_element_type=F32))
    nq = (w_inter * jnp.sum(q * n_ref[hd], axis=1, keepdims=True)
          + jnp.sum(sqk, axis=1, keepdims=True))
    rden = 1.0 / jnp.maximum(jnp.abs(nq), jnp.exp(-m_t))
    ms = jnp.mean(num * num, axis=1, keepdims=True)
    row_scale = rden * lax.rsqrt(ms * rden * rden + NORM_EPS)
    gate = _sigmoid(og_ref[:, vcols].astype(F32)) * _silu(zg_ref[:, vcols].astype(F32))
    o_ref[:, vcols] = (num * row_scale * ng_ref[:, vcols] * gate).astype(o_ref.dtype)

    b_last = b_c[L - 1:L, :]
    g = b_last - b_c + li_c
    m_new = jnp.maximum(b_last + m_prev, jnp.max(g, axis=0, keepdims=True))
    decay = jnp.exp(b_last + m_prev - m_new)
    kw = k * jnp.exp(g - m_new)
    c_ref[hd] = decay * c_ref[hd] + lax.dot_general(
        kw.astype(BF16), vb, (((0,), (0,)), ((), ())), preferred_element_type=F32)
    n_ref[hd] = decay * n_ref[hd] + jnp.sum(kw, axis=0, keepdims=True)
    m_ref[hd] = jnp.broadcast_to(m_new, m_ref.shape[1:])


def _mlstm_kernel(q_ref, k_ref, v_ref, og_ref, zg_ref, gc_ref, gr_ref,
                  wq_ref, wk_ref, ng_ref, o_ref,
                  c_ref, n_ref, m_ref, extq_ref, extk_ref):
    L = q_ref.shape[0]

    @pl.when(pl.program_id(0) == 0)
    def _():
        c_ref[...] = jnp.zeros_like(c_ref)
        n_ref[...] = jnp.zeros_like(n_ref)
        m_ref[...] = jnp.zeros_like(m_ref)
        extq_ref[0:CONV_HALO, :] = jnp.zeros((CONV_HALO, extq_ref.shape[1]), F32)
        extk_ref[0:CONV_HALO, :] = jnp.zeros((CONV_HALO, extk_ref.shape[1]), F32)

    gc = gc_ref[...]
    gr = gr_ref[...]
    row = lax.broadcasted_iota(jnp.int32, (L, L), 0)
    col = lax.broadcasted_iota(jnp.int32, (L, L), 1)
    causal = col <= row
    for hd in range(MLSTM_HEADS):
        _mlstm_head(hd, q_ref, k_ref, v_ref, og_ref, zg_ref, gc, gr, wq_ref, wk_ref,
                    ng_ref, o_ref, c_ref, n_ref, m_ref, extq_ref, extk_ref, causal)


def _mlstm(u, gc, gr, conv_w, norm_g):
    s = u.shape[0]
    L = MLSTM_CHUNK
    H = MLSTM_HEADS
    dk, dv = MLSTM_QK_DIM, MLSTM_V_DIM
    qkw, vw = MLSTM_QK_WIDTH, MLSTM_WIDTH
    q0 = MLSTM_COL0 // qkw
    k0 = q0 + 1
    v0 = (MLSTM_COL0 + 2 * qkw) // vw
    og0 = v0 + 1
    zg0 = og0 + 1
    return pl.pallas_call(
        _mlstm_kernel,
        out_shape=jax.ShapeDtypeStruct((s, vw), BF16),
        grid=(s // L,),
        in_specs=[pl.BlockSpec((L, qkw), lambda c: (c, q0)),
                  pl.BlockSpec((L, qkw), lambda c: (c, k0)),
                  pl.BlockSpec((L, vw), lambda c: (c, v0)),
                  pl.BlockSpec((L, vw), lambda c: (c, og0)),
                  pl.BlockSpec((L, vw), lambda c: (c, zg0)),
                  pl.BlockSpec((L, LANES), lambda c: (c, 0)),
                  pl.BlockSpec((NUM_GATES, L), lambda c: (0, c)),
                  pl.BlockSpec((CONV_WIDTH, qkw), lambda c: (0, 0)),
                  pl.BlockSpec((CONV_WIDTH, qkw), lambda c: (0, 1)),
                  pl.BlockSpec((1, vw), lambda c: (0, 0))],
        out_specs=pl.BlockSpec((L, vw), lambda c: (c, 0)),
        scratch_shapes=[pltpu.VMEM((H, dk, dv), F32),
                        pltpu.VMEM((H, 1, dk), F32),
                        pltpu.VMEM((H, 1, LANES), F32),
                        pltpu.VMEM((CONV_HALO + L, qkw), F32),
                        pltpu.VMEM((CONV_HALO + L, qkw), F32)],
        compiler_params=_params("arbitrary"),
        name="mlstm",
    )(u, u, u, u, u, gc, gr, conv_w, conv_w, norm_g.reshape(1, vw))


def _out_proj_kernel(x_ref, ya_ref, ym_ref, wa_ref, wm_ref, o_ref):
    o_ref[...] = (x_ref[...]
                  + jnp.dot(ya_ref[...], wa_ref[...].astype(BF16), preferred_element_type=F32)
                  + jnp.dot(ym_ref[...], wm_ref[...].astype(BF16), preferred_element_type=F32))


def _out_proj(x, ya, ym, w, layer):
    s, d = x.shape
    ka, km = ya.shape[1], ym.shape[1]
    assert ka == km
    return pl.pallas_call(
        _out_proj_kernel,
        out_shape=jax.ShapeDtypeStruct((s, d), F32),
        grid=(s // OUT_TM, d // OUT_TN),
        in_specs=[pl.BlockSpec((OUT_TM, OUT_TN), lambda i, j: (i, j)),
                  pl.BlockSpec((OUT_TM, ka), lambda i, j: (i, 0), pipeline_mode=RESIDENT),
                  pl.BlockSpec((OUT_TM, km), lambda i, j: (i, 0), pipeline_mode=RESIDENT),
                  pl.BlockSpec((None, ka, OUT_TN), lambda i, j: (layer, 0, j)),
                  pl.BlockSpec((None, km, OUT_TN), lambda i, j: (layer, 1, j))],
        out_specs=pl.BlockSpec((OUT_TM, OUT_TN), lambda i, j: (i, j)),
        compiler_params=_params("parallel", "parallel"),
        name="out_proj",
    )(x, ya, ym, w, w)


def kernel(x, norm_g, w_in, b_gate, conv_w, mlstm_norm_g, w_out, rel_bias, final_g):
    b, s, d = x.shape
    assert b == 1
    xs = x.reshape(s, d)
    near_bias = _near_bias_tiles(rel_bias)
    far_bias = _far_bias_tiles(rel_bias)
    wt_in = jnp.swapaxes(w_in, 1, 2)
    for l in range(DEPTH):
        h, gc, gr = _norm_gates(xs, norm_g[l], wt_in[l, GATE_COL0:], b_gate[l])
        u = _in_proj(h, wt_in, l, GATE_COL0)
        ya = _attention(u, near_bias, far_bias)
        ym = _mlstm(u, gc, gr, conv_w[l], mlstm_norm_g[l])
        xs = _out_proj(xs, ya, ym, w_out, l)
    return _rmsnorm(xs, final_g, F32).reshape(b, s, d)
```

```python
import math

import jax
import jax.numpy as jnp
import numpy as np
from jax import lax
from jax.experimental import pallas as pl
from jax.experimental.pallas import tpu as pltpu

D_MODEL = 4096
DEPTH = 2
ATTN_WIDTH = 2048
ATTN_HEAD_DIM = 128
ATTN_HEADS = ATTN_WIDTH // ATTN_HEAD_DIM
DILATION_PATTERNS = ((128, 1), (512, 4), (2048, 16))
REL_BUCKETS = 32
REL_MAX_EXACT = REL_BUCKETS // 2
REL_MAX_DISTANCE = 2048
MLSTM_WIDTH = 2048
MLSTM_HEADS = 4
MLSTM_V_DIM = MLSTM_WIDTH // MLSTM_HEADS
MLSTM_QK_DIM = MLSTM_V_DIM // 2
MLSTM_QK_WIDTH = MLSTM_HEADS * MLSTM_QK_DIM
CONV_WIDTH = 4
NORM_EPS = 1e-6
ATTN_COLS = 4 * ATTN_WIDTH
MLSTM_COL0 = ATTN_COLS
MLSTM_COLS = 2 * MLSTM_QK_WIDTH + 3 * MLSTM_WIDTH
GATE_COL0 = ATTN_COLS + MLSTM_COLS
NUM_GATES = 2 * MLSTM_HEADS

LANES = 128
SUBLANES = 8
VMEM_LIMIT_BYTES = 56 * 1024 * 1024

NORM_ROWS = 256
MM_TM = 2048
MM_TN = 512
OUT_TM = 2048
OUT_TN = 512
MLSTM_CHUNK = 256
ATTN_TILE = 256
ATTN_SUB = 8
NEAR_MAX_DIST = 512
NEAR_KEY_TILES = NEAR_MAX_DIST // ATTN_TILE + 1
FAR_WINDOW, FAR_DILATION = DILATION_PATTERNS[-1]
FAR_TILE = FAR_WINDOW // FAR_DILATION
assert all(w <= NEAR_MAX_DIST for w, _ in DILATION_PATTERNS[:-1])
assert NEAR_MAX_DIST % FAR_DILATION == 0 and FAR_TILE == LANES
CONV_HALO = SUBLANES
MASKED = -1e30
LOG2E = math.log2(math.e)

F32 = jnp.float32
BF16 = jnp.bfloat16
NT_DIMS = (((1,), (1,)), ((), ()))
RESIDENT = pl.Buffered(1)


def _params(*semantics):
    return pltpu.CompilerParams(dimension_semantics=semantics,
                                vmem_limit_bytes=VMEM_LIMIT_BYTES)


def _sigmoid(x):
    return 1.0 / (1.0 + jnp.exp2(x * (-LOG2E)))


def _silu(x):
    return x * _sigmoid(x)


def _log_sigmoid(x):
    return jnp.minimum(x, 0.0) - jnp.log1p(jnp.exp(-jnp.abs(x)))


def _rmsnorm_kernel(x_ref, g_ref, o_ref):
    x = x_ref[...]
    ms = jnp.mean(x * x, axis=-1, keepdims=True)
    o_ref[...] = (x * lax.rsqrt(ms + NORM_EPS) * g_ref[...]).astype(o_ref.dtype)


def _rmsnorm(x, g, out_dtype):
    s, d = x.shape
    return pl.pallas_call(
        _rmsnorm_kernel,
        out_shape=jax.ShapeDtypeStruct((s, d), out_dtype),
        grid=(s // NORM_ROWS,),
        in_specs=[pl.BlockSpec((NORM_ROWS, d), lambda i: (i, 0)),
                  pl.BlockSpec((1, d), lambda i: (0, 0))],
        out_specs=pl.BlockSpec((NORM_ROWS, d), lambda i: (i, 0)),
        compiler_params=_params("parallel"),
        name="rmsnorm",
    )(x, g.reshape(1, d))


def _matmul_nt_kernel(a_ref, w_ref, o_ref):
    w = w_ref[...].astype(BF16)
    half = a_ref.shape[0] // 2
    for r in range(2):
        rows = slice(r * half, (r + 1) * half)
        o_ref[rows, :] = lax.dot_general(a_ref[rows, :], w, NT_DIMS,
                                         preferred_element_type=F32).astype(o_ref.dtype)


def _in_proj(h, wt, layer, n_cols):
    s, k = h.shape
    return pl.pallas_call(
        _matmul_nt_kernel,
        out_shape=jax.ShapeDtypeStruct((s, n_cols), BF16),
        grid=(s // MM_TM, n_cols // MM_TN),
        in_specs=[pl.BlockSpec((MM_TM, k), lambda i, j: (i, 0), pipeline_mode=RESIDENT),
                  pl.BlockSpec((None, MM_TN, k), lambda i, j: (layer, j, 0))],
        out_specs=pl.BlockSpec((MM_TM, MM_TN), lambda i, j: (i, j)),
        compiler_params=_params("parallel", "parallel"),
        name="in_proj",
    )(h, wt)


def _norm_gates_kernel(x_ref, g_ref, w_ref, bc_ref, br_ref, h_ref, gc_ref, gr_ref):
    L = x_ref.shape[0]
    x = x_ref[...]
    ms = jnp.mean(x * x, axis=-1, keepdims=True)
    h = (x * lax.rsqrt(ms + NORM_EPS) * g_ref[...]).astype(BF16)
    h_ref[...] = h
    w = w_ref[...].astype(BF16)
    gc = lax.dot_general(h, w, NT_DIMS, preferred_element_type=F32) + bc_ref[...]
    gr = lax.dot_general(w[:2 * SUBLANES], h, NT_DIMS,
                         preferred_element_type=F32)[:NUM_GATES] + br_ref[...]
    row = lax.broadcasted_iota(jnp.int32, (L, L), 0)
    col = lax.broadcasted_iota(jnp.int32, (L, L), 1)
    lower = (col <= row).astype(F32)
    upper = (row <= col).astype(F32)

    lane = lax.broadcasted_iota(jnp.int32, gc.shape, 1)
    is_f = (lane >= MLSTM_HEADS) & (lane < NUM_GATES)
    lf_c = jnp.where(is_f, _log_sigmoid(gc), 0.0)
    b_c = jnp.dot(lower, lf_c, preferred_element_type=F32,
                  precision=lax.Precision.HIGHEST)
    gc_ref[...] = jnp.where(is_f, b_c, gc)

    sub = lax.broadcasted_iota(jnp.int32, gr.shape, 0)
    is_f_r = sub >= MLSTM_HEADS
    lf_r = jnp.where(is_f_r, _log_sigmoid(gr), 0.0)
    b_r = jnp.dot(lf_r, upper, preferred_element_type=F32,
                  precision=lax.Precision.HIGHEST)
    gr_ref[...] = jnp.where(is_f_r, b_r, gr)


def _norm_gates(x, g, wt_gate, b_gate):
    s, k = x.shape
    L = MLSTM_CHUNK
    w = jnp.zeros((LANES, k), F32).at[:NUM_GATES].set(wt_gate)
    bc = jnp.zeros((1, LANES), F32).at[0, :NUM_GATES].set(b_gate)
    br = jnp.broadcast_to(b_gate.reshape(NUM_GATES, 1), (NUM_GATES, L))
    return pl.pallas_call(
        _norm_gates_kernel,
        out_shape=(jax.ShapeDtypeStruct((s, k), BF16),
                   jax.ShapeDtypeStruct((s, LANES), F32),
                   jax.ShapeDtypeStruct((NUM_GATES, s), F32)),
        grid=(s // L,),
        in_specs=[pl.BlockSpec((L, k), lambda i: (i, 0)),
                  pl.BlockSpec((1, k), lambda i: (0, 0)),
                  pl.BlockSpec((LANES, k), lambda i: (0, 0)),
                  pl.BlockSpec((1, LANES), lambda i: (0, 0)),
                  pl.BlockSpec((NUM_GATES, L), lambda i: (0, 0))],
        out_specs=(pl.BlockSpec((L, k), lambda i: (i, 0)),
                   pl.BlockSpec((L, LANES), lambda i: (i, 0)),
                   pl.BlockSpec((NUM_GATES, L), lambda i: (0, i))),
        compiler_params=_params("parallel"),
        name="norm_gates",
    )(x, g.reshape(1, k), w, bc, br)


def _rel_buckets_np(dist):
    n = dist.astype(np.float32)
    large = REL_MAX_EXACT + np.floor(
        np.log(np.maximum(n, 1.0) / REL_MAX_EXACT)
        / np.log(REL_MAX_DISTANCE / REL_MAX_EXACT) * (REL_BUCKETS - REL_MAX_EXACT))
    large = np.minimum(large, REL_BUCKETS - 1)
    return np.where(n < REL_MAX_EXACT, n, large).astype(np.int32)


def _attn_bias_rows(rel_bias, tile, key_tiles, unit, lo, hi):
    i = np.arange(2 * tile)
    rel = np.where(i <= tile, -i, 2 * tile - i)
    dist = (np.asarray(key_tiles)[:, None] * tile + rel[None, :]) * unit
    mult = np.zeros(dist.shape, np.int32)
    for w, d in DILATION_PATTERNS:
        mult += ((dist >= lo) & (dist <= min(w, hi)) & (dist % d == 0)).astype(np.int32)
    bucket = _rel_buckets_np(np.maximum(dist, 0))
    logmult = np.log(np.maximum(mult, 1)).astype(np.float32)
    rows = (rel_bias.astype(F32).T[:, bucket] + logmult[None]) * LOG2E
    return jnp.where((mult > 0)[None], rows, MASKED)


def _bias_tiles_kernel(rows_ref, o_ref):
    nt, tile = o_ref.shape[1], o_ref.shape[2]
    for n in range(nt):
        wide = jnp.broadcast_to(rows_ref[0, n:n + 1, :], (tile, 2 * tile))
        o_ref[0, n] = pltpu.roll(wide, 0, 1, stride=1, stride_axis=0)[:, :tile]


def _attn_bias_tiles(rows, name):
    h, nt, two_tile = rows.shape
    tile = two_tile // 2
    return pl.pallas_call(
        _bias_tiles_kernel,
        out_shape=jax.ShapeDtypeStruct((h, nt, tile, tile), F32),
        grid=(h,),
        in_specs=[pl.BlockSpec((1, nt, 2 * tile), lambda i: (i, 0, 0))],
        out_specs=pl.BlockSpec((1, nt, tile, tile), lambda i: (i, 0, 0, 0)),
        compiler_params=_params("parallel"),
        name=name,
    )(rows)


def _near_bias_tiles(rel_bias):
    rows = _attn_bias_rows(rel_bias, ATTN_TILE, range(NEAR_KEY_TILES + 1), 1, 0, NEAR_MAX_DIST)
    return _attn_bias_tiles(rows, "attn_near_bias_tiles")


def _far_bias_tiles(rel_bias):
    rows = _attn_bias_rows(rel_bias, FAR_TILE, (1, 0), FAR_DILATION, NEAR_MAX_DIST + 1, FAR_WINDOW)
    return _attn_bias_tiles(rows, "attn_far_bias_tiles")


def _words_to_rows(w, half):
    bits = (w << 16) if half == 0 else (w & jnp.int32(-65536))
    return pltpu.bitcast(bits, F32)


def _rows_to_words(even, odd):
    lo = lax.shift_right_logical(pltpu.bitcast(even, jnp.int32), 16)
    return lo | (pltpu.bitcast(odd, jnp.int32) & jnp.int32(-65536))


def _attn_far_kernel(q_ref, k_ref, v_ref, bias_ref, o_ref, lse_ref, qw_ref, kw_ref, vw_ref):
    T = FAR_TILE
    D = ATTN_HEAD_DIM
    head = pl.program_id(0)
    n = q_ref.shape[0] // FAR_DILATION
    word_stride = FAR_DILATION // 2

    @pl.when(head == 0)
    def _():
        lse_ref[...] = jnp.zeros_like(lse_ref)

    qw_ref[...] = pltpu.bitcast(q_ref[...], jnp.int32)
    kw_ref[...] = pltpu.bitcast(k_ref[...], jnp.int32)
    vw_ref[...] = pltpu.bitcast(v_ref[...], jnp.int32)
    bias = jnp.concatenate([bias_ref[0, 0], bias_ref[0, 1]], axis=1)
    ones = jnp.ones((2 * T, D), BF16)
    lane = lax.broadcasted_iota(jnp.int32, (n, LANES), 1)

    def residue_pair(rp, carry):
        words = pl.ds(rp, n, stride=word_stride)
        wq, wk, wv = qw_ref[words, :], kw_ref[words, :], vw_ref[words, :]
        outs = []
        for half in range(2):
            q = (_words_to_rows(wq, half) * (D ** -0.5 * LOG2E)).astype(BF16)
            k = _words_to_rows(wk, half).astype(BF16)
            v = _words_to_rows(wv, half).astype(BF16)
            o_blocks, lse_blocks = [], []
            for b in range(n // T):
                k0 = max(b - 1, 0) * T
                nk = (b + 1) * T - k0
                s = lax.dot_general(q[b * T:(b + 1) * T], k[k0:k0 + nk], NT_DIMS,
                                    preferred_element_type=F32) + bias[:, 2 * T - nk:]
                m = jnp.max(s, axis=-1, keepdims=True)
                p = jnp.exp2(s - m).astype(BF16)
                acc = jnp.dot(p, jnp.concatenate([v[k0:k0 + nk], ones[:nk]], axis=1),
                              preferred_element_type=F32)
                o_blocks.append(acc[:, :D] / acc[:, D:])
                lse_blocks.append(m + jnp.log2(acc[:, D:D + 1]))
            lse = jnp.concatenate(lse_blocks, axis=0)
            rows = pl.ds(2 * rp + half, n, stride=FAR_DILATION)
            lse_ref[rows, :] = jnp.where(lane == head, lse, lse_ref[rows, :])
            outs.append(jnp.concatenate(o_blocks, axis=0).astype(BF16).astype(F32))
        o_ref[words, :] = _rows_to_words(outs[0], outs[1])
        return carry

    lax.fori_loop(0, word_stride, residue_pair, 0)


def _attention_far(u, bias_tiles):
    s = u.shape[0]
    H = ATTN_HEADS
    D = ATTN_HEAD_DIM
    T = FAR_TILE
    return pl.pallas_call(
        _attn_far_kernel,
        out_shape=(jax.ShapeDtypeStruct((s // 2, ATTN_WIDTH), jnp.int32),
                   jax.ShapeDtypeStruct((s, LANES), F32)),
        grid=(H,),
        in_specs=[pl.BlockSpec((s, D), lambda h: (0, h)),
                  pl.BlockSpec((s, D), lambda h: (0, H + h)),
                  pl.BlockSpec((s, D), lambda h: (0, 2 * H + h)),
                  pl.BlockSpec((1, 2, T, T), lambda h: (h, 0, 0, 0))],
        out_specs=(pl.BlockSpec((s // 2, D), lambda h: (0, h)),
                   pl.BlockSpec((s, LANES), lambda h: (0, 0))),
        scratch_shapes=[pltpu.VMEM((s // 2, D), jnp.int32)] * 3,
        compiler_params=_params("arbitrary"),
        name="dilated_attention_far",
    )(u, u, u, bias_tiles)


def _attn_near_kernel(q_ref, k_ref, v_ref, z_ref, bias_ref, ofar_ref, lse_ref, o_ref, s_ref):
    T = ATTN_TILE
    D = ATTN_HEAD_DIM
    head = pl.program_id(0)
    step = pl.program_id(1)
    ones = jnp.ones((T, D), BF16)
    lane = lax.broadcasted_iota(jnp.int32, (T, LANES), 1)
    for u in range(ATTN_SUB):
        rows = slice(u * T, (u + 1) * T)
        j = step * ATTN_SUB + u
        q = (q_ref[rows, :].astype(F32) * (D ** -0.5 * LOG2E)).astype(BF16)

        def key_start(kt):
            return pl.multiple_of(jnp.maximum(j - kt, 0) * T, T)

        m = jnp.sum(jnp.where(lane == head, lse_ref[rows, :], 0.0), axis=1, keepdims=True)
        lse_far = m
        for kt in range(NEAR_KEY_TILES):
            bias_idx = jnp.where(kt <= j, kt, NEAR_KEY_TILES)
            k = k_ref[pl.ds(key_start(kt), T), :]
            s = lax.dot_general(q, k, NT_DIMS, preferred_element_type=F32) + bias_ref[0, bias_idx]
            s_ref[u, kt] = s
            m = jnp.maximum(m, jnp.max(s, axis=-1, keepdims=True))

        acc = jnp.zeros((T, 2 * D), F32)
        for kt in range(NEAR_KEY_TILES):
            p = jnp.exp2(s_ref[u, kt] - m).astype(BF16)
            v = v_ref[pl.ds(key_start(kt), T), :]
            acc = acc + jnp.dot(p, jnp.concatenate([v, ones], axis=1),
                                preferred_element_type=F32)
        w_far = jnp.exp2(lse_far - m)
        o_far = pltpu.bitcast(ofar_ref[u * T // 2:(u + 1) * T // 2, :], BF16).astype(F32)
        o = (acc[:, :D] + w_far * o_far) / (acc[:, D:] + w_far)
        o_ref[rows, :] = (o * _silu(z_ref[rows, :].astype(F32))).astype(o_ref.dtype)


def _attention_near(u, bias_tiles, o_far, lse_far):
    s = u.shape[0]
    T = ATTN_TILE
    R = ATTN_SUB * T
    H = ATTN_HEADS
    D = ATTN_HEAD_DIM
    nt = NEAR_KEY_TILES + 1
    return pl.pallas_call(
        _attn_near_kernel,
        out_shape=jax.ShapeDtypeStruct((s, ATTN_WIDTH), BF16),
        grid=(H, s // R),
        in_specs=[pl.BlockSpec((R, D), lambda h, i: (i, h)),
                  pl.BlockSpec((s, D), lambda h, i: (0, H + h)),
                  pl.BlockSpec((s, D), lambda h, i: (0, 2 * H + h)),
                  pl.BlockSpec((R, D), lambda h, i: (i, 3 * H + h)),
                  pl.BlockSpec((1, nt, T, T), lambda h, i: (h, 0, 0, 0)),
                  pl.BlockSpec((R // 2, D), lambda h, i: (i, h)),
                  pl.BlockSpec((R, LANES), lambda h, i: (i, 0))],
        out_specs=pl.BlockSpec((R, D), lambda h, i: (i, h)),
        scratch_shapes=[pltpu.VMEM((ATTN_SUB, NEAR_KEY_TILES, T, T), F32)],
        compiler_params=_params("parallel", "parallel"),
        name="dilated_attention_near",
    )(u, u, u, u, bias_tiles, o_far, lse_far)


def _attention(u, near_bias, far_bias):
    o_far, lse_far = _attention_far(u, far_bias)
    return _attention_near(u, near_bias, o_far, lse_far)


def _conv_silu(ext_ref, x_ref, w_ref, cols):
    L = x_ref.shape[0]
    ext_ref[CONV_HALO:CONV_HALO + L, cols] = x_ref[:, cols].astype(F32)
    acc = None
    for j in range(CONV_WIDTH):
        back = CONV_WIDTH - 1 - j
        term = ext_ref[pl.ds(CONV_HALO - back, L), cols] * w_ref[j:j + 1, cols]
        acc = term if acc is None else acc + term
    ext_ref[0:CONV_HALO, cols] = ext_ref[L:L + CONV_HALO, cols]
    return _silu(acc)


def _mlstm_head(hd, q_ref, k_ref, v_ref, og_ref, zg_ref, gc, gr, wq_ref, wk_ref,
                ng_ref, o_ref, c_ref, n_ref, m_ref, extq_ref, extk_ref, causal):
    L = q_ref.shape[0]
    dk, dv = MLSTM_QK_DIM, MLSTM_V_DIM
    kcols = slice(hd * dk, (hd + 1) * dk)
    vcols = slice(hd * dv, (hd + 1) * dv)

    q = _conv_silu(extq_ref, q_ref, wq_ref, kcols)
    k = _conv_silu(extk_ref, k_ref, wk_ref, kcols) * (dk ** -0.5)
    qb = q.astype(BF16)
    kb = k.astype(BF16)
    vb = v_ref[:, vcols]

    li_c = gc[:, hd:hd + 1]
    b_c = gc[:, MLSTM_HEADS + hd:MLSTM_HEADS + hd + 1]
    li_r = gr[hd:hd + 1, :]
    b_r = gr[MLSTM_HEADS + hd:MLSTM_HEADS + hd + 1, :]

    m_prev = m_ref[hd, :, 0:1]
    dmat = jnp.where(causal, b_c - b_r + li_r, MASKED)
    m_inter = b_c + m_prev
    m_t = jnp.maximum(m_inter, jnp.max(dmat, axis=1, keepdims=True))
    w_inter = jnp.exp(m_inter - m_t)
    sqk = lax.dot_general(qb, kb, NT_DIMS, preferred_element_type=F32) * jnp.exp(dmat - m_t)
    num = (w_inter * jnp.dot(qb, c_ref[hd].astype(BF16), preferred_element_type=F32)
           + jnp.dot(sqk.astype(BF16), vb, preferred_element_type=F32))
    nq = (w_inter * jnp.sum(q * n_ref[hd], axis=1, keepdims=True)
          + jnp.sum(sqk, axis=1, keepdims=True))
    rden = 1.0 / jnp.maximum(jnp.abs(nq), jnp.exp(-m_t))
    ms = jnp.mean(num * num, axis=1, keepdims=True)
    row_scale = rden * lax.rsqrt(ms * rden * rden + NORM_EPS)
    gate = _sigmoid(og_ref[:, vcols].astype(F32)) * _silu(zg_ref[:, vcols].astype(F32))
    o_ref[:, vcols] = (num * row_scale * ng_ref[:, vcols] * gate).astype(o_ref.dtype)

    b_last = b_c[L - 1:L, :]
    g = b_last - b_c + li_c
    m_new = jnp.maximum(b_last + m_prev, jnp.max(g, axis=0, keepdims=True))
    decay = jnp.exp(b_last + m_prev - m_new)
    kw = k * jnp.exp(g - m_new)
    c_ref[hd] = decay * c_ref[hd] + lax.dot_general(
        kw.astype(BF16), vb, (((0,), (0,)), ((), ())), preferred_element_type=F32)
    n_ref[hd] = decay * n_ref[hd] + jnp.sum(kw, axis=0, keepdims=True)
    m_ref[hd] = jnp.broadcast_to(m_new, m_ref.shape[1:])


def _mlstm_kernel(q_ref, k_ref, v_ref, og_ref, zg_ref, gc_ref, gr_ref,
                  wq_ref, wk_ref, ng_ref, o_ref,
                  c_ref, n_ref, m_ref, extq_ref, extk_ref):
    L = q_ref.shape[0]

    @pl.when(pl.program_id(0) == 0)
    def _():
        c_ref[...] = jnp.zeros_like(c_ref)
        n_ref[...] = jnp.zeros_like(n_ref)
        m_ref[...] = jnp.zeros_like(m_ref)
        extq_ref[0:CONV_HALO, :] = jnp.zeros((CONV_HALO, extq_ref.shape[1]), F32)
        extk_ref[0:CONV_HALO, :] = jnp.zeros((CONV_HALO, extk_ref.shape[1]), F32)

    gc = gc_ref[...]
    gr = gr_ref[...]
    row = lax.broadcasted_iota(jnp.int32, (L, L), 0)
    col = lax.broadcasted_iota(jnp.int32, (L, L), 1)
    causal = col <= row
    for hd in range(MLSTM_HEADS):
        _mlstm_head(hd, q_ref, k_ref, v_ref, og_ref, zg_ref, gc, gr, wq_ref, wk_ref,
                    ng_ref, o_ref, c_ref, n_ref, m_ref, extq_ref, extk_ref, causal)


def _mlstm(u, gc, gr, conv_w, norm_g):
    s = u.shape[0]
    L = MLSTM_CHUNK
    H = MLSTM_HEADS
    dk, dv = MLSTM_QK_DIM, MLSTM_V_DIM
    qkw, vw = MLSTM_QK_WIDTH, MLSTM_WIDTH
    q0 = MLSTM_COL0 // qkw
    k0 = q0 + 1
    v0 = (MLSTM_COL0 + 2 * qkw) // vw
    og0 = v0 + 1
    zg0 = og0 + 1
    return pl.pallas_call(
        _mlstm_kernel,
        out_shape=jax.ShapeDtypeStruct((s, vw), BF16),
        grid=(s // L,),
        in_specs=[pl.BlockSpec((L, qkw), lambda c: (c, q0)),
                  pl.BlockSpec((L, qkw), lambda c: (c, k0)),
                  pl.BlockSpec((L, vw), lambda c: (c, v0)),
                  pl.BlockSpec((L, vw), lambda c: (c, og0)),
                  pl.BlockSpec((L, vw), lambda c: (c, zg0)),
                  pl.BlockSpec((L, LANES), lambda c: (c, 0)),
                  pl.BlockSpec((NUM_GATES, L), lambda c: (0, c)),
                  pl.BlockSpec((CONV_WIDTH, qkw), lambda c: (0, 0)),
                  pl.BlockSpec((CONV_WIDTH, qkw), lambda c: (0, 1)),
                  pl.BlockSpec((1, vw), lambda c: (0, 0))],
        out_specs=pl.BlockSpec((L, vw), lambda c: (c, 0)),
        scratch_shapes=[pltpu.VMEM((H, dk, dv), F32),
                        pltpu.VMEM((H, 1, dk), F32),
                        pltpu.VMEM((H, 1, LANES), F32),
                        pltpu.VMEM((CONV_HALO + L, qkw), F32),
                        pltpu.VMEM((CONV_HALO + L, qkw), F32)],
        compiler_params=_params("arbitrary"),
        name="mlstm",
    )(u, u, u, u, u, gc, gr, conv_w, conv_w, norm_g.reshape(1, vw))


def _out_proj_kernel(x_ref, ya_ref, ym_ref, wa_ref, wm_ref, o_ref):
    wa = wa_ref[...].astype(BF16)
    wm = wm_ref[...].astype(BF16)
    half = x_ref.shape[0] // 2
    for r in range(2):
        rows = slice(r * half, (r + 1) * half)
        o_ref[rows, :] = (x_ref[rows, :]
                          + jnp.dot(ya_ref[rows, :], wa, preferred_element_type=F32)
                          + jnp.dot(ym_ref[rows, :], wm, preferred_element_type=F32))


def _out_proj(x, ya, ym, w, layer):
    s, d = x.shape
    ka, km = ya.shape[1], ym.shape[1]
    assert ka == km
    return pl.pallas_call(
        _out_proj_kernel,
        out_shape=jax.ShapeDtypeStruct((s, d), F32),
        grid=(s // OUT_TM, d // OUT_TN),
        in_specs=[pl.BlockSpec((OUT_TM, OUT_TN), lambda i, j: (i, j)),
                  pl.BlockSpec((OUT_TM, ka), lambda i, j: (i, 0), pipeline_mode=RESIDENT),
                  pl.BlockSpec((OUT_TM, km), lambda i, j: (i, 0), pipeline_mode=RESIDENT),
                  pl.BlockSpec((None, ka, OUT_TN), lambda i, j: (layer, 0, j)),
                  pl.BlockSpec((None, km, OUT_TN), lambda i, j: (layer, 1, j))],
        out_specs=pl.BlockSpec((OUT_TM, OUT_TN), lambda i, j: (i, j)),
        compiler_params=_params("parallel", "parallel"),
        name="out_proj",
    )(x, ya, ym, w, w)


def kernel(x, norm_g, w_in, b_gate, conv_w, mlstm_norm_g, w_out, rel_bias, final_g):
    b, s, d = x.shape
    assert b == 1
    xs = x.reshape(s, d)
    near_bias = _near_bias_tiles(rel_bias)
    far_bias = _far_bias_tiles(rel_bias)
    wt_in = jnp.swapaxes(w_in, 1, 2)
    for l in range(DEPTH):
        h, gc, gr = _norm_gates(xs, norm_g[l], wt_in[l, GATE_COL0:], b_gate[l])
        u = _in_proj(h, wt_in, l, GATE_COL0)
        ya = _attention(u, near_bias, far_bias)
        ym = _mlstm(u, gc, gr, conv_w[l], mlstm_norm_g[l])
        xs = _out_proj(xs, ya, ym, w_out, l)
    return _rmsnorm(xs, final_g, F32).reshape(b, s, d)
```

```python
import math

import jax
import jax.numpy as jnp
import numpy as np
from jax import lax
from jax.experimental import pallas as pl
from jax.experimental.pallas import tpu as pltpu

D_MODEL = 4096
DEPTH = 2
ATTN_WIDTH = 2048
ATTN_HEAD_DIM = 128
ATTN_HEADS = ATTN_WIDTH // ATTN_HEAD_DIM
DILATION_PATTERNS = ((128, 1), (512, 4), (2048, 16))
REL_BUCKETS = 32
REL_MAX_EXACT = REL_BUCKETS // 2
REL_MAX_DISTANCE = 2048
MLSTM_WIDTH = 2048
MLSTM_HEADS = 4
MLSTM_V_DIM = MLSTM_WIDTH // MLSTM_HEADS
MLSTM_QK_DIM = MLSTM_V_DIM // 2
MLSTM_QK_WIDTH = MLSTM_HEADS * MLSTM_QK_DIM
CONV_WIDTH = 4
NORM_EPS = 1e-6
ATTN_COLS = 4 * ATTN_WIDTH
MLSTM_COL0 = ATTN_COLS
MLSTM_COLS = 2 * MLSTM_QK_WIDTH + 3 * MLSTM_WIDTH
GATE_COL0 = ATTN_COLS + MLSTM_COLS
NUM_GATES = 2 * MLSTM_HEADS

LANES = 128
SUBLANES = 8
BF16_BITS = 16
BF16_TILE_ROWS = 2 * SUBLANES
VMEM_LIMIT_BYTES = 56 * 1024 * 1024

NORM_ROWS = 512
MM_TM = 2048
MM_TN = 512
OUT_TM = 2048
OUT_TN = 512
MLSTM_CHUNK = 128
NORM_GATES_ROWS = 512
assert NORM_GATES_ROWS % MLSTM_CHUNK == 0
ATTN_TILE = 256
NEAR_SCORE_SLOTS = 4
NEAR_MAX_DIST = 512
NEAR_KEY_TILES = NEAR_MAX_DIST // ATTN_TILE + 1
FAR_WINDOW, FAR_DILATION = DILATION_PATTERNS[-1]
FAR_TILE = FAR_WINDOW // FAR_DILATION
assert all(w <= NEAR_MAX_DIST for w, _ in DILATION_PATTERNS[:-1])
assert NEAR_MAX_DIST % FAR_DILATION == 0 and FAR_TILE == LANES
assert ATTN_TILE == 2 * ATTN_HEAD_DIM
CONV_HALO = SUBLANES
MASKED = -1e30
LOG2E = math.log2(math.e)

F32 = jnp.float32
BF16 = jnp.bfloat16
NT_DIMS = (((1,), (1,)), ((), ()))
RESIDENT = pl.Buffered(1)


def _params(*semantics):
    return pltpu.CompilerParams(dimension_semantics=semantics,
                                vmem_limit_bytes=VMEM_LIMIT_BYTES)


def _sigmoid(x):
    return 1.0 / (1.0 + jnp.exp2(x * (-LOG2E)))


def _silu(x):
    return x * _sigmoid(x)


def _log_sigmoid(x):
    return jnp.minimum(x, 0.0) - jnp.log1p(jnp.exp(-jnp.abs(x)))


def _rmsnorm_kernel(x_ref, g_ref, o_ref):
    x = x_ref[...]
    ms = jnp.mean(x * x, axis=-1, keepdims=True)
    o_ref[...] = (x * lax.rsqrt(ms + NORM_EPS) * g_ref[...]).astype(o_ref.dtype)


def _rmsnorm(x, g, out_dtype):
    s, d = x.shape
    return pl.pallas_call(
        _rmsnorm_kernel,
        out_shape=jax.ShapeDtypeStruct((s, d), out_dtype),
        grid=(s // NORM_ROWS,),
        in_specs=[pl.BlockSpec((NORM_ROWS, d), lambda i: (i, 0)),
                  pl.BlockSpec((1, d), lambda i: (0, 0))],
        out_specs=pl.BlockSpec((NORM_ROWS, d), lambda i: (i, 0)),
        compiler_params=_params("parallel"),
        name="rmsnorm",
    )(x, g.reshape(1, d))


def _matmul_nt_kernel(a_ref, w_ref, o_ref):
    w = w_ref[...].astype(BF16)
    half = a_ref.shape[0] // 2
    for r in range(2):
        rows = slice(r * half, (r + 1) * half)
        o_ref[rows, :] = lax.dot_general(a_ref[rows, :], w, NT_DIMS,
                                         preferred_element_type=F32).astype(o_ref.dtype)


def _in_proj(h, wt, layer, col0, n_cols):
    s, k = h.shape
    j0 = col0 // MM_TN
    return pl.pallas_call(
        _matmul_nt_kernel,
        out_shape=jax.ShapeDtypeStruct((s, n_cols), BF16),
        grid=(s // MM_TM, n_cols // MM_TN),
        in_specs=[pl.BlockSpec((MM_TM, k), lambda i, j: (i, 0)),
                  pl.BlockSpec((None, MM_TN, k), lambda i, j: (layer, j0 + j, 0))],
        out_specs=pl.BlockSpec((MM_TM, MM_TN), lambda i, j: (i, j)),
        compiler_params=_params("parallel", "parallel"),
        name="in_proj",
    )(h, wt)


def _norm_gates_kernel(x_ref, g_ref, w_ref, bc_ref, br_ref, h_ref, gc_ref, gr_ref):
    L = MLSTM_CHUNK
    x = x_ref[...]
    ms = jnp.mean(x * x, axis=-1, keepdims=True)
    h = (x * lax.rsqrt(ms + NORM_EPS) * g_ref[...]).astype(BF16)
    h_ref[...] = h
    w = w_ref[...].astype(BF16)
    gc = lax.dot_general(h, w, NT_DIMS, preferred_element_type=F32) + bc_ref[...]
    gr = lax.dot_general(w[:BF16_TILE_ROWS], h, NT_DIMS,
                         preferred_element_type=F32)[:NUM_GATES] + br_ref[...]
    row = lax.broadcasted_iota(jnp.int32, (L, L), 0)
    col = lax.broadcasted_iota(jnp.int32, (L, L), 1)
    lower = (col <= row).astype(F32)
    upper = (row <= col).astype(F32)

    lane = lax.broadcasted_iota(jnp.int32, (L, LANES), 1)
    is_f = (lane >= MLSTM_HEADS) & (lane < NUM_GATES)
    sub = lax.broadcasted_iota(jnp.int32, (NUM_GATES, L), 0)
    is_f_r = sub >= MLSTM_HEADS
    for c in range(x_ref.shape[0] // L):
        rows = slice(c * L, (c + 1) * L)
        gc_c = gc[rows]
        lf_c = jnp.where(is_f, _log_sigmoid(gc_c), 0.0)
        b_c = jnp.dot(lower, lf_c, preferred_element_type=F32,
                      precision=lax.Precision.HIGHEST)
        gc_ref[rows, :] = jnp.where(is_f, b_c, gc_c)
        gr_c = gr[:, rows]
        lf_r = jnp.where(is_f_r, _log_sigmoid(gr_c), 0.0)
        b_r = jnp.dot(lf_r, upper, preferred_element_type=F32,
                      precision=lax.Precision.HIGHEST)
        gr_ref[:, rows] = jnp.where(is_f_r, b_r, gr_c)


def _norm_gates(x, g, wt_gate, b_gate):
    s, k = x.shape
    L = NORM_GATES_ROWS
    w = jnp.zeros((LANES, k), F32).at[:NUM_GATES].set(wt_gate)
    bc = jnp.zeros((1, LANES), F32).at[0, :NUM_GATES].set(b_gate)
    br = jnp.broadcast_to(b_gate.reshape(NUM_GATES, 1), (NUM_GATES, L))
    return pl.pallas_call(
        _norm_gates_kernel,
        out_shape=(jax.ShapeDtypeStruct((s, k), BF16),
                   jax.ShapeDtypeStruct((s, LANES), F32),
                   jax.ShapeDtypeStruct((NUM_GATES, s), F32)),
        grid=(s // L,),
        in_specs=[pl.BlockSpec((L, k), lambda i: (i, 0)),
                  pl.BlockSpec((1, k), lambda i: (0, 0)),
                  pl.BlockSpec((LANES, k), lambda i: (0, 0)),
                  pl.BlockSpec((1, LANES), lambda i: (0, 0)),
                  pl.BlockSpec((NUM_GATES, L), lambda i: (0, 0))],
        out_specs=(pl.BlockSpec((L, k), lambda i: (i, 0)),
                   pl.BlockSpec((L, LANES), lambda i: (i, 0)),
                   pl.BlockSpec((NUM_GATES, L), lambda i: (0, i))),
        compiler_params=_params("parallel"),
        name="norm_gates",
    )(x, g.reshape(1, k), w, bc, br)


def _rel_buckets_np(dist):
    n = dist.astype(np.float32)
    large = REL_MAX_EXACT + np.floor(
        np.log(np.maximum(n, 1.0) / REL_MAX_EXACT)
        / np.log(REL_MAX_DISTANCE / REL_MAX_EXACT) * (REL_BUCKETS - REL_MAX_EXACT))
    large = np.minimum(large, REL_BUCKETS - 1)
    return np.where(n < REL_MAX_EXACT, n, large).astype(np.int32)


def _attn_bias_rows(rel_bias, tile, key_tiles, unit, lo, hi):
    i = np.arange(2 * tile)
    rel = np.where(i <= tile, -i, 2 * tile - i)
    dist = (np.asarray(key_tiles)[:, None] * tile + rel[None, :]) * unit
    mult = np.zeros(dist.shape, np.int32)
    for w, d in DILATION_PATTERNS:
        mult += ((dist >= lo) & (dist <= min(w, hi)) & (dist % d == 0)).astype(np.int32)
    bucket = _rel_buckets_np(np.maximum(dist, 0))
    logmult = np.log(np.maximum(mult, 1)).astype(np.float32)
    rows = (rel_bias.astype(F32).T[:, bucket] + logmult[None]) * LOG2E
    return jnp.where((mult > 0)[None], rows, MASKED)


def _bias_tiles_kernel(rows_ref, o_ref):
    nt, tile = o_ref.shape[1], o_ref.shape[2]
    for n in range(nt):
        wide = jnp.broadcast_to(rows_ref[0, n:n + 1, :], (tile, 2 * tile))
        o_ref[0, n] = pltpu.roll(wide, 0, 1, stride=1, stride_axis=0)[:, :tile]


def _attn_bias_tiles(rows, name):
    h, nt, two_tile = rows.shape
    tile = two_tile // 2
    return pl.pallas_call(
        _bias_tiles_kernel,
        out_shape=jax.ShapeDtypeStruct((h, nt, tile, tile), F32),
        grid=(h,),
        in_specs=[pl.BlockSpec((1, nt, 2 * tile), lambda i: (i, 0, 0))],
        out_specs=pl.BlockSpec((1, nt, tile, tile), lambda i: (i, 0, 0, 0)),
        compiler_params=_params("parallel"),
        name=name,
    )(rows)


def _near_bias_tiles(rel_bias):
    rows = _attn_bias_rows(rel_bias, ATTN_TILE, range(NEAR_KEY_TILES), 1, 0, NEAR_MAX_DIST)
    return _attn_bias_tiles(rows, "attn_near_bias_tiles")


def _far_bias_tiles(rel_bias):
    rows = _attn_bias_rows(rel_bias, FAR_TILE, (1, 0), FAR_DILATION, NEAR_MAX_DIST + 1, FAR_WINDOW)
    return _attn_bias_tiles(rows, "attn_far_bias_tiles")


def _words_to_rows(w, half):
    bits = (w << BF16_BITS) if half == 0 else (w & jnp.int32(-(1 << BF16_BITS)))
    return pltpu.bitcast(bits, F32)


def _attn_far_part(q_ref, k_ref, v_ref, bias_ref, o_ref, lse_ref, qw_ref, kw_ref, vw_ref):
    T = FAR_TILE
    D = ATTN_HEAD_DIM
    n = q_ref.shape[0] // FAR_DILATION
    word_stride = FAR_DILATION // 2

    qw_ref[...] = pltpu.bitcast(q_ref[...], jnp.int32)
    kw_ref[...] = pltpu.bitcast(k_ref[...], jnp.int32)
    vw_ref[...] = pltpu.bitcast(v_ref[...], jnp.int32)
    bias = jnp.concatenate([bias_ref[0, 0], bias_ref[0, 1]], axis=1)
    ones = jnp.ones((2 * T, D), BF16)

    def residue_pair(rp, carry):
        words = pl.ds(rp, n, stride=word_stride)
        wq, wk, wv = qw_ref[words, :], kw_ref[words, :], vw_ref[words, :]
        for half in range(2):
            q = (_words_to_rows(wq, half) * (D ** -0.5 * LOG2E)).astype(BF16)
            k = _words_to_rows(wk, half).astype(BF16)
            v = _words_to_rows(wv, half).astype(BF16)
            o_blocks, lse_blocks = [], []
            for b in range(n // T):
                k0 = max(b - 1, 0) * T
                nk = (b + 1) * T - k0
                s = lax.dot_general(q[b * T:(b + 1) * T], k[k0:k0 + nk], NT_DIMS,
                                    preferred_element_type=F32) + bias[:, 2 * T - nk:]
                m = jnp.max(s, axis=-1, keepdims=True)
                p = jnp.exp2(s - m).astype(BF16)
                acc = jnp.dot(p, jnp.concatenate([v[k0:k0 + nk], ones[:nk]], axis=1),
                              preferred_element_type=F32)
                o_blocks.append(acc[:, :D] / acc[:, D:])
                lse_blocks.append(m + jnp.log2(acc[:, D:]))
            rows = pl.ds(2 * rp + half, n, stride=FAR_DILATION)
            o_ref[rows, :] = jnp.concatenate(o_blocks, axis=0)
            lse_ref[rows, :] = jnp.concatenate(lse_blocks, axis=0)
        return carry

    lax.fori_loop(0, word_stride, residue_pair, 0, unroll=word_stride)


def _attn_near_part(q_ref, k_ref, v_ref, z_ref, bias_ref, ofar_ref, lse_ref, o_ref, s_ref):
    T = ATTN_TILE
    D = ATTN_HEAD_DIM
    ones = jnp.ones((T, D), BF16)
    for j in range(q_ref.shape[0] // T):
        rows = slice(j * T, (j + 1) * T)
        tiles = range(max(j - (NEAR_KEY_TILES - 1), 0), j + 1)
        slot = j % NEAR_SCORE_SLOTS
        q = (q_ref[rows, :].astype(F32) * (D ** -0.5 * LOG2E)).astype(BF16)

        m_near = None
        for t in tiles:
            k = k_ref[t * T:(t + 1) * T, :]
            s = lax.dot_general(q, k, NT_DIMS, preferred_element_type=F32) + bias_ref[0, j - t]
            s_ref[slot, j - t] = s
            mt = jnp.max(s, axis=-1, keepdims=True)
            m_near = mt if m_near is None else jnp.maximum(m_near, mt)
        lse_far = lse_ref[rows, :]
        m = jnp.maximum(m_near, lse_far)
        m_tile = jnp.concatenate([m, m], axis=1)

        acc = jnp.zeros((T, 2 * D), F32)
        for t in tiles:
            p = jnp.exp2(s_ref[slot, j - t] - m_tile).astype(BF16)
            v = v_ref[t * T:(t + 1) * T, :]
            acc = acc + jnp.dot(p, jnp.concatenate([v, ones], axis=1),
                                preferred_element_type=F32)
        w_far = jnp.exp2(lse_far - m)
        o = (acc[:, :D] + w_far * ofar_ref[rows, :]) / (acc[:, D:] + w_far)
        o_ref[rows, :] = (o * _silu(z_ref[rows, :].astype(F32))).astype(o_ref.dtype)


def _attn_kernel(q_ref, k_ref, v_ref, z_ref, near_bias_ref, far_bias_ref, o_ref,
                 s_ref, ofar_ref, lse_ref, qw_ref, kw_ref, vw_ref):
    _attn_far_part(q_ref, k_ref, v_ref, far_bias_ref, ofar_ref, lse_ref, qw_ref, kw_ref, vw_ref)
    _attn_near_part(q_ref, k_ref, v_ref, z_ref, near_bias_ref, ofar_ref, lse_ref, o_ref, s_ref)


def _attention(u, near_bias, far_bias):
    s = u.shape[0]
    T = ATTN_TILE
    H = ATTN_HEADS
    D = ATTN_HEAD_DIM
    nt = NEAR_KEY_TILES
    head_cols = lambda first: pl.BlockSpec((s, D), lambda h: (0, first + h))
    return pl.pallas_call(
        _attn_kernel,
        out_shape=jax.ShapeDtypeStruct((s, ATTN_WIDTH), BF16),
        grid=(H,),
        in_specs=[head_cols(0), head_cols(H), head_cols(2 * H), head_cols(3 * H),
                  pl.BlockSpec((1, nt, T, T), lambda h: (h, 0, 0, 0)),
                  pl.BlockSpec((1, 2, FAR_TILE, FAR_TILE), lambda h: (h, 0, 0, 0))],
        out_specs=head_cols(0),
        scratch_shapes=[pltpu.VMEM((NEAR_SCORE_SLOTS, nt, T, T), F32),
                        pltpu.VMEM((s, D), F32),
                        pltpu.VMEM((s, D), F32)]
                       + [pltpu.VMEM((s // 2, D), jnp.int32)] * 3,
        compiler_params=_params("parallel"),
        name="dilated_attention",
    )(u, u, u, u, near_bias, far_bias)


def _conv_silu(ext_ref, x_ref, w_ref, cols):
    L = x_ref.shape[0]
    ext_ref[CONV_HALO:CONV_HALO + L, cols] = x_ref[:, cols].astype(F32)
    acc = None
    for j in range(CONV_WIDTH):
        back = CONV_WIDTH - 1 - j
        term = ext_ref[pl.ds(CONV_HALO - back, L), cols] * w_ref[j:j + 1, cols]
        acc = term if acc is None else acc + term
    ext_ref[0:CONV_HALO, cols] = ext_ref[L:L + CONV_HALO, cols]
    return _silu(acc)


def _mlstm_head(hd, q_ref, k_ref, v_ref, og_ref, zg_ref, gc, gr, wq_ref, wk_ref,
                ng_ref, o_ref, c_ref, n_ref, m_ref, extq_ref, extk_ref, causal):
    L = q_ref.shape[0]
    dk, dv = MLSTM_QK_DIM, MLSTM_V_DIM
    kcols = slice(hd * dk, (hd + 1) * dk)
    vcols = slice(hd * dv, (hd + 1) * dv)

    q = _conv_silu(extq_ref, q_ref, wq_ref, kcols)
    k = _conv_silu(extk_ref, k_ref, wk_ref, kcols) * (dk ** -0.5)
    qb = q.astype(BF16)
    kb = k.astype(BF16)
    vb = v_ref[:, vcols]

    li_c = gc[:, hd:hd + 1]
    b_c = gc[:, MLSTM_HEADS + hd:MLSTM_HEADS + hd + 1]
    li_r = gr[hd:hd + 1, :]
    b_r = gr[MLSTM_HEADS + hd:MLSTM_HEADS + hd + 1, :]

    m_prev = m_ref[hd, :, 0:1]
    dmat = jnp.where(causal, b_c - b_r + li_r, MASKED)
    m_inter = b_c + m_prev
    m_t = jnp.maximum(m_inter, jnp.max(dmat, axis=1, keepdims=True))
    w_inter = jnp.exp(m_inter - m_t)
    sqk = lax.dot_general(qb, kb, NT_DIMS, preferred_element_type=F32) * jnp.exp(dmat - m_t)
    num = (w_inter * jnp.dot(qb, c_ref[hd].astype(BF16), preferred_element_type=F32)
           + jnp.dot(sqk.astype(BF16), vb, preferred_element_type=F32))
    nq = (w_inter * jnp.sum(q * n_ref[hd], axis=1, keepdims=True)
          + jnp.sum(sqk, axis=1, keepdims=True))
    rden = 1.0 / jnp.maximum(jnp.abs(nq), jnp.exp(-m_t))
    ms = jnp.mean(num * num, axis=1, keepdims=True)
    row_scale = rden * lax.rsqrt(ms * rden * rden + NORM_EPS)
    gate = _sigmoid(og_ref[:, vcols].astype(F32)) * _silu(zg_ref[:, vcols].astype(F32))
    o_ref[:, vcols] = (num * row_scale * ng_ref[:, vcols] * gate).astype(o_ref.dtype)

    b_last = b_c[L - 1:L, :]
    g = b_last - b_c + li_c
    m_new = jnp.maximum(b_last + m_prev, jnp.max(g, axis=0, keepdims=True))
    decay = jnp.exp(b_last + m_prev - m_new)
    kw = k * jnp.exp(g - m_new)
    c_ref[hd] = decay * c_ref[hd] + lax.dot_general(
        kw.astype(BF16), vb, (((0,), (0,)), ((), ())), preferred_element_type=F32)
    n_ref[hd] = decay * n_ref[hd] + jnp.sum(kw, axis=0, keepdims=True)
    m_ref[hd] = jnp.broadcast_to(m_new, m_ref.shape[1:])


def _mlstm_reset(c_ref, n_ref, m_ref, extq_ref, extk_ref):
    c_ref[...] = jnp.zeros_like(c_ref)
    n_ref[...] = jnp.zeros_like(n_ref)
    m_ref[...] = jnp.zeros_like(m_ref)
    extq_ref[0:CONV_HALO, :] = jnp.zeros((CONV_HALO, extq_ref.shape[1]), F32)
    extk_ref[0:CONV_HALO, :] = jnp.zeros((CONV_HALO, extk_ref.shape[1]), F32)


def _mlstm_chunk(q_ref, k_ref, v_ref, og_ref, zg_ref, gc_ref, gr_ref,
                 wq_ref, wk_ref, ng_ref, o_ref,
                 c_ref, n_ref, m_ref, extq_ref, extk_ref):
    L = q_ref.shape[0]
    gc = gc_ref[...]
    gr = gr_ref[...]
    row = lax.broadcasted_iota(jnp.int32, (L, L), 0)
    col = lax.broadcasted_iota(jnp.int32, (L, L), 1)
    causal = col <= row
    for hd in range(MLSTM_HEADS):
        _mlstm_head(hd, q_ref, k_ref, v_ref, og_ref, zg_ref, gc, gr, wq_ref, wk_ref,
                    ng_ref, o_ref, c_ref, n_ref, m_ref, extq_ref, extk_ref, causal)


def _in_proj_mlstm_kernel(a_ref, w_ref, q_ref, k_ref, v_ref, og_ref, zg_ref, gc_ref, gr_ref,
                          wq_ref, wk_ref, ng_ref, u_ref, ym_ref,
                          c_ref, n_ref, m_ref, extq_ref, extk_ref):
    first = (pl.program_id(0) == 0) & (pl.program_id(1) == 0)

    @pl.when(first)
    def _():
        _mlstm_reset(c_ref, n_ref, m_ref, extq_ref, extk_ref)

    _matmul_nt_kernel(a_ref, w_ref, u_ref)
    _mlstm_chunk(q_ref, k_ref, v_ref, og_ref, zg_ref, gc_ref, gr_ref,
                 wq_ref, wk_ref, ng_ref, ym_ref, c_ref, n_ref, m_ref, extq_ref, extk_ref)


def _in_proj_mlstm(h, wt, layer, um, gc, gr, conv_w, norm_g):
    s, k = h.shape
    L = MLSTM_CHUNK
    H = MLSTM_HEADS
    dk, dv = MLSTM_QK_DIM, MLSTM_V_DIM
    qkw, vw = MLSTM_QK_WIDTH, MLSTM_WIDTH
    n_i, n_j = s // MM_TM, ATTN_COLS // MM_TN
    assert n_i * n_j == s // L
    v0 = 2 * qkw // vw
    chunk = lambda i, j: i * n_j + j
    return pl.pallas_call(
        _in_proj_mlstm_kernel,
        out_shape=(jax.ShapeDtypeStruct((s, ATTN_COLS), BF16),
                   jax.ShapeDtypeStruct((s, vw), BF16)),
        grid=(n_i, n_j),
        in_specs=[pl.BlockSpec((MM_TM, k), lambda i, j: (i, 0), pipeline_mode=RESIDENT),
                  pl.BlockSpec((None, MM_TN, k), lambda i, j: (layer, j, 0)),
                  pl.BlockSpec((L, qkw), lambda i, j: (chunk(i, j), 0)),
                  pl.BlockSpec((L, qkw), lambda i, j: (chunk(i, j), 1)),
                  pl.BlockSpec((L, vw), lambda i, j: (chunk(i, j), v0)),
                  pl.BlockSpec((L, vw), lambda i, j: (chunk(i, j), v0 + 1)),
                  pl.BlockSpec((L, vw), lambda i, j: (chunk(i, j), v0 + 2)),
                  pl.BlockSpec((L, LANES), lambda i, j: (chunk(i, j), 0)),
                  pl.BlockSpec((NUM_GATES, L), lambda i, j: (0, chunk(i, j))),
                  pl.BlockSpec((CONV_WIDTH, qkw), lambda i, j: (0, 0)),
                  pl.BlockSpec((CONV_WIDTH, qkw), lambda i, j: (0, 1)),
                  pl.BlockSpec((1, vw), lambda i, j: (0, 0))],
        out_specs=(pl.BlockSpec((MM_TM, MM_TN), lambda i, j: (i, j)),
                   pl.BlockSpec((L, vw), lambda i, j: (chunk(i, j), 0))),
        scratch_shapes=[pltpu.VMEM((H, dk, dv), F32),
                        pltpu.VMEM((H, 1, dk), F32),
                        pltpu.VMEM((H, 1, LANES), F32),
                        pltpu.VMEM((CONV_HALO + L, qkw), F32),
                        pltpu.VMEM((CONV_HALO + L, qkw), F32)],
        compiler_params=_params("arbitrary", "arbitrary"),
        name="in_proj_mlstm",
    )(h, wt, um, um, um, um, um, gc, gr, conv_w, conv_w, norm_g.reshape(1, vw))


def _out_proj_kernel(x_ref, ya_ref, ym_ref, wa_ref, wm_ref, o_ref):
    wa = wa_ref[...].astype(BF16)
    wm = wm_ref[...].astype(BF16)
    half = x_ref.shape[0] // 2
    for r in range(2):
        rows = slice(r * half, (r + 1) * half)
        o_ref[rows, :] = (x_ref[rows, :]
                          + jnp.dot(ya_ref[rows, :], wa, preferred_element_type=F32)
                          + jnp.dot(ym_ref[rows, :], wm, preferred_element_type=F32))


def _out_proj(x, ya, ym, w, layer):
    s, d = x.shape
    ka, km = ya.shape[1], ym.shape[1]
    assert ka == km
    return pl.pallas_call(
        _out_proj_kernel,
        out_shape=jax.ShapeDtypeStruct((s, d), F32),
        grid=(s // OUT_TM, d // OUT_TN),
        in_specs=[pl.BlockSpec((OUT_TM, OUT_TN), lambda i, j: (i, j)),
                  pl.BlockSpec((OUT_TM, ka), lambda i, j: (i, 0), pipeline_mode=RESIDENT),
                  pl.BlockSpec((OUT_TM, km), lambda i, j: (i, 0), pipeline_mode=RESIDENT),
                  pl.BlockSpec((None, ka, OUT_TN), lambda i, j: (layer, 0, j)),
                  pl.BlockSpec((None, km, OUT_TN), lambda i, j: (layer, 1, j))],
        out_specs=pl.BlockSpec((OUT_TM, OUT_TN), lambda i, j: (i, j)),
        compiler_params=_params("parallel", "parallel"),
        name="out_proj",
    )(x, ya, ym, w, w)


def kernel(x, norm_g, w_in, b_gate, conv_w, mlstm_norm_g, w_out, rel_bias, final_g):
    b, s, d = x.shape
    assert b == 1 and d == D_MODEL
    xs = x.reshape(s, d)
    near_bias = _near_bias_tiles(rel_bias)
    far_bias = _far_bias_tiles(rel_bias)
    wt_in = jnp.swapaxes(w_in, 1, 2)
    for l in range(DEPTH):
        h, gc, gr = _norm_gates(xs, norm_g[l], wt_in[l, GATE_COL0:], b_gate[l])
        um = _in_proj(h, wt_in, l, MLSTM_COL0, MLSTM_COLS)
        ua, ym = _in_proj_mlstm(h, wt_in, l, um, gc, gr, conv_w[l], mlstm_norm_g[l])
        ya = _attention(ua, near_bias, far_bias)
        xs = _out_proj(xs, ya, ym, w_out, l)
    return _rmsnorm(xs, final_g, F32).reshape(b, s, d)
```
